```python
import jax, jax.numpy as jnp
from jax import lax
import numpy as np

D_MODEL = 2048
BATCH = 8
SEQ = 2048
DEPTH = 1
DEC_BATCH = 32
DEC_SEQ = 4
PAST_LEN = 8192
PAGE_SIZE = 128

MIX_DIM = D_MODEL
CONV_DIM = MIX_DIM // 2
ATTN_DIM = MIX_DIM - CONV_DIM
HEAD_DIM = 128
N_HEADS = ATTN_DIM // HEAD_DIM
CONV_W = 3
D_FF = 256 * ((8 * D_MODEL // 3 + 255) // 256)
PLE_DIM = 256
Q_BLOCK = 128
N_IN = 3 * CONV_DIM + 3 * ATTN_DIM + N_HEADS
EPS = 1e-6
SCALE = HEAD_DIM ** -0.5

kernel_name = "hymba_fox_shortconv_macaron_step"


def rms_norm(x, g):
    xf = x.astype(jnp.float32)
    y = xf * lax.rsqrt(jnp.mean(xf * xf, axis=-1, keepdims=True) + EPS)
    return (y * g.astype(jnp.float32)).astype(x.dtype)


def ffn_half(h, g, wg, wu, wd):
    hn = rms_norm(h, g)
    return h + 0.5 * ((jax.nn.silu(hn @ wg) * (hn @ wu)) @ wd)


def ple_add(h, p, g, wg, wp):
    hn = rms_norm(h, g)
    return h + jax.nn.sigmoid(hn @ wg) * (p.astype(h.dtype) @ wp)


def mixer_in(h, g, w_in, b_f):
    hn = rms_norm(h, g)
    z = hn @ w_in
    idx = [CONV_DIM, 2 * CONV_DIM, 3 * CONV_DIM,
           3 * CONV_DIM + ATTN_DIM, 3 * CONV_DIM + 2 * ATTN_DIM, 3 * CONV_DIM + 3 * ATTN_DIM]
    cb, cc, ch, q, k, v, fl = jnp.split(z, idx, axis=-1)
    lead = h.shape[:-1]
    q = q.reshape(*lead, N_HEADS, HEAD_DIM)
    k = k.reshape(*lead, N_HEADS, HEAD_DIM)
    v = v.reshape(*lead, N_HEADS, HEAD_DIM)
    logf = jax.nn.log_sigmoid(fl.astype(jnp.float32) + b_f.astype(jnp.float32))
    return cb, cc * ch, q, k, v, logf


def short_conv(u_ext, w):
    T = u_ext.shape[1] - (CONV_W - 1)
    y = w[0] * u_ext[:, 0:T]
    for j in range(1, CONV_W):
        y = y + w[j] * u_ext[:, j:j + T]
    return y


def mixer_out(h, conv_y, attn_o, gc, ga, w_out):
    lead = attn_o.shape[:-2]
    z = jnp.concatenate([rms_norm(conv_y, gc),
                         rms_norm(attn_o.reshape(*lead, ATTN_DIM), ga)], axis=-1)
    return h + z @ w_out


def fox_prompt(q, k, v, logf):
    B_, S_ = q.shape[0], q.shape[1]
    nb = S_ // Q_BLOCK
    F = jnp.cumsum(logf, axis=1)
    Fk = F.transpose(0, 2, 1)
    kpos = jnp.arange(S_)
    qb = q.reshape(B_, nb, Q_BLOCK, N_HEADS, HEAD_DIM).transpose(1, 0, 2, 3, 4)
    Fq = F.reshape(B_, nb, Q_BLOCK, N_HEADS).transpose(1, 0, 3, 2)
    qpos = jnp.arange(S_).reshape(nb, Q_BLOCK)

    def block(args):
        q_i, F_i, pos_i = args
        s = jnp.einsum('bqhd,bkhd->bhqk', q_i, k, preferred_element_type=jnp.float32) * SCALE
        s = s + F_i[..., None] - Fk[:, :, None, :]
        s = jnp.where(kpos[None, None, None, :] <= pos_i[None, None, :, None], s, -jnp.inf)
        p = jax.nn.softmax(s, axis=-1)
        return jnp.einsum('bhqk,bkhd->bqhd', p.astype(v.dtype), v)

    o = lax.map(block, (qb, Fq, qpos))
    return o.transpose(1, 0, 2, 3, 4).reshape(B_, S_, N_HEADS, HEAD_DIM)


def fox_sample(q, k_new, v_new, logf_new, cache_k, cache_v, cache_logf, page_table):
    DB, T = q.shape[0], q.shape[1]
    P = page_table.shape[1] * PAGE_SIZE
    k_past = cache_k[page_table].reshape(DB, P, N_HEADS, HEAD_DIM)
    v_past = cache_v[page_table].reshape(DB, P, N_HEADS, HEAD_DIM)
    lf_past = cache_logf[page_table].reshape(DB, P, N_HEADS)
    k_all = jnp.concatenate([k_past.astype(k_new.dtype), k_new], axis=1)
    v_all = jnp.concatenate([v_past.astype(v_new.dtype), v_new], axis=1)
    lf_all = jnp.concatenate([lf_past.astype(jnp.float32), logf_new], axis=1)
    F = jnp.cumsum(lf_all, axis=1).transpose(0, 2, 1)
    Fq = F[:, :, P:]
    s = jnp.einsum('bqhd,bkhd->bhqk', q, k_all, preferred_element_type=jnp.float32) * SCALE
    s = s + Fq[..., None] - F[:, :, None, :]
    qpos = P + jnp.arange(T)
    kpos = jnp.arange(P + T)
    s = jnp.where(kpos[None, None, None, :] <= qpos[None, None, :, None], s, -jnp.inf)
    p = jax.nn.softmax(s, axis=-1)
    return jnp.einsum('bhqk,bkhd->bqhd', p.astype(v_all.dtype), v_all)


def setup_inputs(seed: int = 0) -> dict:
    key = jax.random.key(seed)
    ks = jax.random.split(key, 32)
    n_pages = PAST_LEN // PAGE_SIZE
    n_used = DEC_BATCH * n_pages
    n_pool = n_used + max(1, n_used // 4)
    nrm = jax.random.normal
    f32 = jnp.float32
    page_table = jax.random.permutation(ks[0], n_pool)[:n_used].reshape(DEC_BATCH, n_pages).astype(jnp.int32)

    def gain(k, n):
        return 1.0 + 0.05 * nrm(k, (DEPTH, n), f32)

    return {
        "x_prompt": nrm(ks[1], (BATCH, SEQ, D_MODEL), f32),
        "x_sample": nrm(ks[2], (DEC_BATCH, DEC_SEQ, D_MODEL), f32),
        "p_prompt": nrm(ks[3], (DEPTH, BATCH, SEQ, PLE_DIM), f32),
        "p_sample": nrm(ks[4], (DEPTH, DEC_BATCH, DEC_SEQ, PLE_DIM), f32),
        "cache_k": nrm(ks[5], (DEPTH, n_pool, PAGE_SIZE, N_HEADS, HEAD_DIM), f32),
        "cache_v": nrm(ks[6], (DEPTH, n_pool, PAGE_SIZE, N_HEADS, HEAD_DIM), f32),
        "cache_logf": jax.nn.log_sigmoid(2.0 + nrm(ks[7], (DEPTH, n_pool, PAGE_SIZE, N_HEADS), f32)),
        "state_conv": nrm(ks[8], (DEPTH, DEC_BATCH, CONV_W - 1, CONV_DIM), f32),
        "page_table": page_table,
        "norm_ffn1": gain(ks[9], D_MODEL),
        "w_ffn1_gate": nrm(ks[10], (DEPTH, D_MODEL, D_FF), f32) * D_MODEL ** -0.5,
        "w_ffn1_up": nrm(ks[11], (DEPTH, D_MODEL, D_FF), f32) * D_MODEL ** -0.5,
        "w_ffn1_down": nrm(ks[12], (DEPTH, D_FF, D_MODEL), f32) * D_FF ** -0.5,
        "norm_mix": gain(ks[13], D_MODEL),
        "w_in": nrm(ks[14], (DEPTH, D_MODEL, N_IN), f32) * D_MODEL ** -0.5,
        "b_f": 2.0 + 0.5 * nrm(ks[15], (DEPTH, N_HEADS), f32),
        "conv_w": nrm(ks[16], (DEPTH, CONV_W, CONV_DIM), f32) * CONV_W ** -0.5,
        "norm_conv_out": gain(ks[17], CONV_DIM),
        "norm_attn_out": gain(ks[18], ATTN_DIM),
        "w_out": nrm(ks[19], (DEPTH, MIX_DIM, D_MODEL), f32) * MIX_DIM ** -0.5,
        "norm_ffn2": gain(ks[20], D_MODEL),
        "w_ffn2_gate": nrm(ks[21], (DEPTH, D_MODEL, D_FF), f32) * D_MODEL ** -0.5,
        "w_ffn2_up": nrm(ks[22], (DEPTH, D_MODEL, D_FF), f32) * D_MODEL ** -0.5,
        "w_ffn2_down": nrm(ks[23], (DEPTH, D_FF, D_MODEL), f32) * D_FF ** -0.5,
        "norm_ple": gain(ks[24], D_MODEL),
        "w_ple_gate": nrm(ks[25], (DEPTH, D_MODEL, D_MODEL), f32) * D_MODEL ** -0.5,
        "w_ple_proj": nrm(ks[26], (DEPTH, PLE_DIM, D_MODEL), f32) * PLE_DIM ** -0.5,
        "norm_final": 1.0 + 0.05 * nrm(ks[27], (D_MODEL,), f32),
    }


def reference(x_prompt, x_sample, p_prompt, p_sample, cache_k, cache_v, cache_logf, state_conv,
              page_table, norm_ffn1, w_ffn1_gate, w_ffn1_up, w_ffn1_down, norm_mix, w_in, b_f,
              conv_w, norm_conv_out, norm_attn_out, w_out, norm_ffn2, w_ffn2_gate, w_ffn2_up,
              w_ffn2_down, norm_ple, w_ple_gate, w_ple_proj, norm_final):
    hp, hs = x_prompt, x_sample
    kp_l, vp_l, lfp_l, cp_l = [], [], [], []
    ks_l, vs_l, lfs_l, cs_l = [], [], [], []
    for l in range(DEPTH):
        hp = ffn_half(hp, norm_ffn1[l], w_ffn1_gate[l], w_ffn1_up[l], w_ffn1_down[l])
        cb, u, q, k, v, logf = mixer_in(hp, norm_mix[l], w_in[l], b_f[l])
        u_ext = jnp.pad(u, ((0, 0), (CONV_W - 1, 0), (0, 0)))
        conv_y = cb * short_conv(u_ext, conv_w[l])
        attn_o = fox_prompt(q, k, v, logf)
        hp = mixer_out(hp, conv_y, attn_o, norm_conv_out[l], norm_attn_out[l], w_out[l])
        hp = ffn_half(hp, norm_ffn2[l], w_ffn2_gate[l], w_ffn2_up[l], w_ffn2_down[l])
        hp = ple_add(hp, p_prompt[l], norm_ple[l], w_ple_gate[l], w_ple_proj[l])
        kp_l.append(k)
        vp_l.append(v)
        lfp_l.append(logf)
        cp_l.append(u_ext[:, -(CONV_W - 1):])

        hs = ffn_half(hs, norm_ffn1[l], w_ffn1_gate[l], w_ffn1_up[l], w_ffn1_down[l])
        cb, u, q, k, v, logf = mixer_in(hs, norm_mix[l], w_in[l], b_f[l])
        u_ext = jnp.concatenate([state_conv[l].astype(u.dtype), u], axis=1)
        conv_y = cb * short_conv(u_ext, conv_w[l])
        attn_o = fox_sample(q, k, v, logf, cache_k[l], cache_v[l], cache_logf[l], page_table)
        hs = mixer_out(hs, conv_y, attn_o, norm_conv_out[l], norm_attn_out[l], w_out[l])
        hs = ffn_half(hs, norm_ffn2[l], w_ffn2_gate[l], w_ffn2_up[l], w_ffn2_down[l])
        hs = ple_add(hs, p_sample[l], norm_ple[l], w_ple_gate[l], w_ple_proj[l])
        ks_l.append(k)
        vs_l.append(v)
        lfs_l.append(logf)
        cs_l.append(u_ext[:, -(CONV_W - 1):])

    y_prompt = rms_norm(hp, norm_final)
    y_sample = rms_norm(hs, norm_final)
    return (y_prompt, y_sample,
            jnp.stack(kp_l), jnp.stack(vp_l), jnp.stack(lfp_l), jnp.stack(cp_l),
            jnp.stack(ks_l), jnp.stack(vs_l), jnp.stack(lfs_l), jnp.stack(cs_l))
```

```python
import functools

import jax
import jax.numpy as jnp
from jax import lax
from jax.experimental import pallas as pl
from jax.experimental.pallas import tpu as pltpu

F32 = jnp.float32
BF16 = jnp.bfloat16

EPS = 1e-6
HEAD_DIM = 128
PAGE_SIZE = 128
CONV_W = 3
LANES = 128
SUBLANES = 8
MASK_VALUE = -1e30
VMEM_LIMIT = 56 * 1024 * 1024

TOKEN_TILE = 512
FF_TILE = 512
ATTN_BLOCK = 256
PAGES_PER_STEP = 8


def _params(*sem):
    return pltpu.CompilerParams(dimension_semantics=sem, vmem_limit_bytes=VMEM_LIMIT)


def _rms(x, g):
    ms = jnp.mean(x * x, axis=-1, keepdims=True)
    return x * lax.rsqrt(ms + EPS) * g


def _dot(a, b):
    return jnp.dot(a, b, preferred_element_type=F32)


def _dot_nt(a, b):
    return lax.dot_general(a, b, (((1,), (1,)), ((), ())), preferred_element_type=F32)


def _log_sigmoid(x):
    return jnp.minimum(x, 0.0) - jnp.log1p(jnp.exp(-jnp.abs(x)))


def _ffn_kernel(x_ref, g_ref, wg_ref, wu_ref, wd_ref, o_ref, hn_ref, acc_ref):
    j = pl.program_id(1)

    @pl.when(j == 0)
    def _():
        hn_ref[...] = _rms(x_ref[...], g_ref[...]).astype(BF16)
        acc_ref[...] = jnp.zeros_like(acc_ref)

    hn = hn_ref[...]
    gate = _dot(hn, wg_ref[...])
    up = _dot(hn, wu_ref[...])
    act = (gate * jax.nn.sigmoid(gate) * up).astype(BF16)
    acc_ref[...] += _dot(act, wd_ref[...])

    @pl.when(j == pl.num_programs(1) - 1)
    def _():
        o_ref[...] = x_ref[...] + 0.5 * acc_ref[...]


def _ffn(x, g, wg, wu, wd, tm):
    t, d = x.shape
    f = wg.shape[1]
    tf = FF_TILE
    return pl.pallas_call(
        _ffn_kernel,
        grid=(t // tm, f // tf),
        in_specs=[
            pl.BlockSpec((tm, d), lambda i, j: (i, 0)),
            pl.BlockSpec((1, d), lambda i, j: (0, 0)),
            pl.BlockSpec((d, tf), lambda i, j: (0, j)),
            pl.BlockSpec((d, tf), lambda i, j: (0, j)),
            pl.BlockSpec((tf, d), lambda i, j: (j, 0)),
        ],
        out_specs=pl.BlockSpec((tm, d), lambda i, j: (i, 0)),
        out_shape=jax.ShapeDtypeStruct((t, d), F32),
        scratch_shapes=[pltpu.VMEM((tm, d), BF16), pltpu.VMEM((tm, d), F32)],
        compiler_params=_params("parallel", "arbitrary"),
        name="ffn_half",
    )(x, g, wg, wu, wd)


def _mixin_common(j, x_ref, g_ref, w_ref, wf_ref, bf_ref, lf_ref, hn_ref, n_heads):
    @pl.when(j == 0)
    def _():
        hn = _rms(x_ref[...], g_ref[...]).astype(BF16)
        hn_ref[...] = hn
        fl = _dot(hn, wf_ref[...])
        lf_ref[...] = _log_sigmoid(fl[:, :n_heads] + bf_ref[...])

    return _dot(hn_ref[...], w_ref[...])


def _conv_taps(u, uext_ref, tm):
    uext_ref[pl.ds(SUBLANES, tm), :] = u
    um1 = uext_ref[pl.ds(SUBLANES - 1, tm), :]
    um2 = uext_ref[pl.ds(SUBLANES - 2, tm), :]
    return um1, um2


def _mixin_prompt_kernel(x_ref, g_ref, w_ref, wf_ref, bf_ref, cw_ref, gc_ref,
                         zc_ref, q_ref, k_ref, v_ref, kb_ref, vb_ref, lf_ref, cs_ref,
                         hn_ref, cb_ref, cc_ref, uext_ref, *, tm, tiles_per_seq, n_heads, scale):
    i = pl.program_id(0)
    j = pl.program_id(1)
    z = _mixin_common(j, x_ref, g_ref, w_ref, wf_ref, bf_ref, lf_ref, hn_ref, n_heads)

    @pl.when(j == 0)
    def _():
        cb_ref[...] = z

    @pl.when(j == 1)
    def _():
        cc_ref[...] = z

    @pl.when(j == 2)
    def _():
        @pl.when(i % tiles_per_seq == 0)
        def _():
            uext_ref[pl.ds(0, SUBLANES), :] = jnp.zeros((SUBLANES, z.shape[1]), F32)

        u = cc_ref[...] * z
        um1, um2 = _conv_taps(u, uext_ref, tm)
        y = cb_ref[...] * (cw_ref[0:1, :] * um2 + cw_ref[1:2, :] * um1 + cw_ref[2:3, :] * u)
        zc_ref[...] = _rms(y, gc_ref[...]).astype(BF16)
        cs_ref[0] = u[tm - (CONV_W - 1):, :]
        uext_ref[pl.ds(0, SUBLANES), :] = u[tm - SUBLANES:, :]

    @pl.when(j == 3)
    def _():
        q_ref[...] = (z * scale).astype(BF16)

    @pl.when(j == 4)
    def _():
        k_ref[...] = z
        kb_ref[...] = z.astype(BF16)

    @pl.when(j == 5)
    def _():
        v_ref[...] = z
        vb_ref[...] = z.astype(BF16)


def _mixin_prompt(x, g, w_main, w_f, b_f, cw, gc, seq, n_heads, scale):
    t, d = x.shape
    c = w_main.shape[1] // 6
    tm = TOKEN_TILE
    row = lambda i, j: (i, 0)
    fixed = lambda i, j: (0, 0)
    kern = functools.partial(_mixin_prompt_kernel, tm=tm, tiles_per_seq=seq // tm,
                             n_heads=n_heads, scale=scale)
    return pl.pallas_call(
        kern,
        grid=(t // tm, 6),
        in_specs=[
            pl.BlockSpec((tm, d), row),
            pl.BlockSpec((1, d), fixed),
            pl.BlockSpec((d, c), lambda i, j: (0, j)),
            pl.BlockSpec((d, LANES), fixed),
            pl.BlockSpec((1, n_heads), fixed),
            pl.BlockSpec((CONV_W, c), fixed),
            pl.BlockSpec((1, c), fixed),
        ],
        out_specs=[
            pl.BlockSpec((tm, c), row),
            pl.BlockSpec((tm, c), row),
            pl.BlockSpec((tm, c), row),
            pl.BlockSpec((tm, c), row),
            pl.BlockSpec((tm, c), row),
            pl.BlockSpec((tm, c), row),
            pl.BlockSpec((tm, n_heads), row),
            pl.BlockSpec((1, CONV_W - 1, c), lambda i, j: (i // (seq // tm), 0, 0)),
        ],
        out_shape=[
            jax.ShapeDtypeStruct((t, c), BF16),
            jax.ShapeDtypeStruct((t, c), BF16),
            jax.ShapeDtypeStruct((t, c), F32),
            jax.ShapeDtypeStruct((t, c), F32),
            jax.ShapeDtypeStruct((t, c), BF16),
            jax.ShapeDtypeStruct((t, c), BF16),
            jax.ShapeDtypeStruct((t, n_heads), F32),
            jax.ShapeDtypeStruct((t // seq, CONV_W - 1, c), F32),
        ],
        scratch_shapes=[
            pltpu.VMEM((tm, d), BF16),
            pltpu.VMEM((tm, c), F32),
            pltpu.VMEM((tm, c), F32),
            pltpu.VMEM((tm + SUBLANES, c), F32),
        ],
        compiler_params=_params("arbitrary", "arbitrary"),
        name="mixer_in_prompt",
    )(x, g, w_main, w_f, b_f, cw, gc)


def _mixin_sample_kernel(x_ref, g_ref, w_ref, wf_ref, bf_ref, cw_ref, gc_ref, s1_ref, s2_ref,
                         zc_ref, q_ref, k_ref, v_ref, u_ref, lf_ref,
                         hn_ref, cb_ref, cc_ref, uext_ref, *, tm, dec_seq, n_heads, scale):
    j = pl.program_id(1)
    z = _mixin_common(j, x_ref, g_ref, w_ref, wf_ref, bf_ref, lf_ref, hn_ref, n_heads)

    @pl.when(j == 0)
    def _():
        cb_ref[...] = z

    @pl.when(j == 1)
    def _():
        cc_ref[...] = z

    @pl.when(j == 2)
    def _():
        uext_ref[pl.ds(0, SUBLANES), :] = jnp.zeros((SUBLANES, z.shape[1]), F32)
        u = cc_ref[...] * z
        um1, um2 = _conv_taps(u, uext_ref, tm)
        step = lax.broadcasted_iota(jnp.int32, u.shape, 0) % dec_seq
        um1 = jnp.where(step >= 1, um1, s1_ref[...])
        um2 = jnp.where(step >= 2, um2, s2_ref[...])
        y = cb_ref[...] * (cw_ref[0:1, :] * um2 + cw_ref[1:2, :] * um1 + cw_ref[2:3, :] * u)
        zc_ref[...] = _rms(y, gc_ref[...]).astype(BF16)
        u_ref[...] = u

    @pl.when(j == 3)
    def _():
        q_ref[...] = z * scale

    @pl.when(j == 4)
    def _():
        k_ref[...] = z

    @pl.when(j == 5)
    def _():
        v_ref[...] = z


def _mixin_sample(x, g, w_main, w_f, b_f, cw, gc, s1, s2, dec_seq, n_heads, scale):
    t, d = x.shape
    c = w_main.shape[1] // 6
    tm = t
    row = lambda i, j: (i, 0)
    fixed = lambda i, j: (0, 0)
    kern = functools.partial(_mixin_sample_kernel, tm=tm, dec_seq=dec_seq, n_heads=n_heads,
                             scale=scale)
    return pl.pallas_call(
        kern,
        grid=(1, 6),
        in_specs=[
            pl.BlockSpec((tm, d), row),
            pl.BlockSpec((1, d), fixed),
            pl.BlockSpec((d, c), lambda i, j: (0, j)),
            pl.BlockSpec((d, LANES), fixed),
            pl.BlockSpec((1, n_heads), fixed),
            pl.BlockSpec((CONV_W, c), fixed),
            pl.BlockSpec((1, c), fixed),
            pl.BlockSpec((tm, c), row),
            pl.BlockSpec((tm, c), row),
        ],
        out_specs=[
            pl.BlockSpec((tm, c), row),
            pl.BlockSpec((tm, c), row),
            pl.BlockSpec((tm, c), row),
            pl.BlockSpec((tm, c), row),
            pl.BlockSpec((tm, c), row),
            pl.BlockSpec((tm, n_heads), row),
        ],
        out_shape=[
            jax.ShapeDtypeStruct((t, c), BF16),
            jax.ShapeDtypeStruct((t, c), F32),
            jax.ShapeDtypeStruct((t, c), F32),
            jax.ShapeDtypeStruct((t, c), F32),
            jax.ShapeDtypeStruct((t, c), F32),
            jax.ShapeDtypeStruct((t, n_heads), F32),
        ],
        scratch_shapes=[
            pltpu.VMEM((tm, d), BF16),
            pltpu.VMEM((tm, c), F32),
            pltpu.VMEM((tm, c), F32),
            pltpu.VMEM((tm + SUBLANES, c), F32),
        ],
        compiler_params=_params("arbitrary", "arbitrary"),
        name="mixer_in_sample",
    )(x, g, w_main, w_f, b_f, cw, gc, s1, s2)


def _lane_cumsum(x):
    n = x.shape[-1]
    lane = lax.broadcasted_iota(jnp.int32, x.shape, x.ndim - 1)
    shift = 1
    while shift < n:
        x = x + jnp.where(lane >= shift, pltpu.roll(x, shift, x.ndim - 1), 0.0)
        shift *= 2
    return x


def _attn_prompt_kernel(q_ref, k_ref, v_ref, lft_ref, ga_ref, za_ref, f_ref, o_ref,
                        *, blk, n_heads):
    i = pl.program_id(1)
    n_blocks = f_ref.shape[0]

    @pl.when(i == 0)
    def _():
        f = _lane_cumsum(lft_ref[0])
        for jb in range(n_blocks):
            f_ref[jb] = f[:, jb * blk:(jb + 1) * blk]

    row = lax.broadcasted_iota(jnp.int32, (blk, blk), 0)
    col = lax.broadcasted_iota(jnp.int32, (blk, blk), 1)
    causal = col <= row

    def step(qh, h, jb, carry, masked):
        m, l, acc = carry
        hs = slice(h * HEAD_DIM, (h + 1) * HEAD_DIM)
        kj = k_ref[pl.ds(pl.multiple_of(jb * blk, blk), blk), hs]
        vj = v_ref[pl.ds(pl.multiple_of(jb * blk, blk), blk), hs]
        s = _dot_nt(qh, kj) - f_ref[jb, h:h + 1, :]
        if masked:
            s = jnp.where(causal, s, MASK_VALUE)
        m_new = jnp.maximum(m, jnp.max(s, axis=-1, keepdims=True))
        alpha = jnp.exp(m - m_new)
        p = jnp.exp(s - m_new)
        l = alpha * l + jnp.sum(p, axis=-1, keepdims=True)
        acc = alpha * acc + _dot(p.astype(BF16), vj)
        return m_new, l, acc

    for h in range(n_heads):
        qh = q_ref[:, h * HEAD_DIM:(h + 1) * HEAD_DIM]
        init = (jnp.full((blk, 1), MASK_VALUE, F32), jnp.zeros((blk, 1), F32),
                jnp.zeros((blk, HEAD_DIM), F32))
        carry = lax.fori_loop(0, i, lambda jb, c: step(qh, h, jb, c, False), init)
        m, l, acc = step(qh, h, i, carry, True)
        o_ref[:, h * HEAD_DIM:(h + 1) * HEAD_DIM] = acc / l

    za_ref[...] = _rms(o_ref[...], ga_ref[...]).astype(BF16)


def _attn_prompt(q, kb, vb, lft, ga, n_heads):
    t, c = q.shape
    b, _, seq = lft.shape
    blk = ATTN_BLOCK
    nq = seq // blk
    kern = functools.partial(_attn_prompt_kernel, blk=blk, n_heads=n_heads)
    return pl.pallas_call(
        kern,
        grid=(b, nq),
        in_specs=[
            pl.BlockSpec((blk, c), lambda bi, i: (bi * nq + i, 0)),
            pl.BlockSpec((seq, c), lambda bi, i: (bi, 0)),
            pl.BlockSpec((seq, c), lambda bi, i: (bi, 0)),
            pl.BlockSpec((1, n_heads, seq), lambda bi, i: (bi, 0, 0)),
            pl.BlockSpec((1, c), lambda bi, i: (0, 0)),
        ],
        out_specs=pl.BlockSpec((blk, c), lambda bi, i: (bi * nq + i, 0)),
        out_shape=jax.ShapeDtypeStruct((t, c), BF16),
        scratch_shapes=[pltpu.VMEM((nq, n_heads, blk), F32), pltpu.VMEM((blk, c), F32)],
        compiler_params=_params("arbitrary", "arbitrary"),
        name="fox_prompt",
    )(q, kb, vb, lft, ga)


def _split3_dot(x, w):
    hi = x.astype(BF16)
    r1 = x - hi.astype(F32)
    mid = r1.astype(BF16)
    lo = (r1 - mid.astype(F32)).astype(BF16)
    return _dot(hi, w) + _dot(mid, w) + _dot(lo, w)


def _select_sum_kernel(x_ref, w_ref, o_ref):
    o_ref[...] = _split3_dot(x_ref[...], w_ref[...])


def _select_sum(x, w, tm):
    rows, k = x.shape
    n = w.shape[1]
    return pl.pallas_call(
        _select_sum_kernel,
        grid=(rows // tm,),
        in_specs=[pl.BlockSpec((tm, k), lambda i: (i, 0)), pl.BlockSpec((k, n), lambda i: (0, 0))],
        out_specs=pl.BlockSpec((tm, n), lambda i: (i, 0)),
        out_shape=jax.ShapeDtypeStruct((rows, n), F32),
        compiler_params=_params("parallel"),
        name="head_major_prefix_sum",
    )(x, w)


def _prefix_matrix(n_steps, n_heads, width):
    r = jnp.arange(n_steps * n_heads)
    cidx = jnp.arange(n_heads * width)
    t, h = r // n_heads, r % n_heads
    h2, t2 = cidx // width, cidx % width
    w = (h[:, None] == h2[None, :]) & (t[:, None] <= t2[None, :]) & (t2[None, :] < n_steps)
    return w.astype(BF16)


def _attn_sample_kernel(pt_ref, q_ref, kn_ref, vn_ref, fn_ref, ga_ref, *rest,
                        pages, n_heads, dec_seq):
    k_refs = rest[:pages]
    v_refs = rest[pages:2 * pages]
    pf_refs = rest[2 * pages:3 * pages]
    za_ref = rest[3 * pages]
    qbd_ref, kb_ref, vb_ref, m_ref, l_ref, acc_ref, run_ref = rest[3 * pages + 1:]
    g = pl.program_id(1)
    rows = dec_seq * n_heads
    c = n_heads * HEAD_DIM
    head_of_row = lax.broadcasted_iota(jnp.int32, (n_heads, c), 0)
    head_of_col = lax.broadcasted_iota(jnp.int32, (n_heads, c), 1) // HEAD_DIM
    diag = head_of_row == head_of_col

    @pl.when(g == 0)
    def _():
        for t in range(dec_seq):
            qt = jnp.broadcast_to(q_ref[0, t:t + 1, :], (n_heads, c))
            qbd_ref[t * n_heads:(t + 1) * n_heads, :] = jnp.where(diag, qt, 0.0).astype(BF16)
        m_ref[...] = jnp.full(m_ref.shape, MASK_VALUE, F32)
        l_ref[...] = jnp.zeros(l_ref.shape, F32)
        acc_ref[...] = jnp.zeros(acc_ref.shape, F32)
        run_ref[...] = jnp.zeros(run_ref.shape, F32)

    def online_update(s, v_bf):
        m_old = m_ref[...]
        m_new = jnp.maximum(m_old, jnp.max(s, axis=-1, keepdims=True))
        alpha = jnp.exp(m_old - m_new)
        p = jnp.exp(s - m_new)
        l_ref[...] = alpha * l_ref[...] + jnp.sum(p, axis=-1, keepdims=True)
        acc_ref[...] = alpha * acc_ref[...] + _dot(p.astype(BF16), v_bf)
        m_ref[...] = m_new

    run = run_ref[...]
    bias_parts = []
    for pg in range(pages):
        kb_ref[pg * PAGE_SIZE:(pg + 1) * PAGE_SIZE, :] = k_refs[pg][0].astype(BF16)
        vb_ref[pg * PAGE_SIZE:(pg + 1) * PAGE_SIZE, :] = v_refs[pg][0].astype(BF16)
        pf = pf_refs[pg][0]
        bias_parts.append(run + pf)
        run = run + jnp.broadcast_to(pf[:, PAGE_SIZE - 1:PAGE_SIZE], pf.shape)
    run_ref[...] = run
    bias = jnp.concatenate(bias_parts, axis=1)
    bias = jnp.concatenate([bias] * dec_seq, axis=0)
    s = _dot_nt(qbd_ref[...], kb_ref[...]) - bias
    online_update(s, vb_ref[...])

    @pl.when(g == pl.num_programs(1) - 1)
    def _():
        pad = jnp.zeros((PAGE_SIZE - SUBLANES, c), F32)
        kn = jnp.concatenate([kn_ref[0], pad], axis=0).astype(BF16)
        vn = jnp.concatenate([vn_ref[0], pad], axis=0).astype(BF16)
        bias_n = jnp.concatenate([run + fn_ref[0]] * dec_seq, axis=0)
        s_n = _dot_nt(qbd_ref[...], kn) - bias_n
        q_step = lax.broadcasted_iota(jnp.int32, (rows, PAGE_SIZE), 0) // n_heads
        k_step = lax.broadcasted_iota(jnp.int32, (rows, PAGE_SIZE), 1)
        s_n = jnp.where(k_step <= q_step, s_n, MASK_VALUE)
        online_update(s_n, vn)
        o = acc_ref[...] / l_ref[...]
        for t in range(dec_seq):
            ot = jnp.where(diag, o[t * n_heads:(t + 1) * n_heads, :], 0.0)
            ot = jnp.sum(ot, axis=0, keepdims=True)
            za_ref[0, t:t + 1, :] = _rms(ot, ga_ref[...])


def _attn_sample(page_table, q, k_new, v_new, f_new, ga, cache_k, cache_v, pfx, n_heads, dec_seq):
    nb, n_pages = page_table.shape
    c = n_heads * HEAD_DIM
    pages = PAGES_PER_STEP
    rows = dec_seq * n_heads
    per_b = lambda b, g, pt: (b, 0, 0)
    page = lambda r: (lambda b, g, pt: (pt[b, g * pages + r], 0, 0))
    kern = functools.partial(_attn_sample_kernel, pages=pages, n_heads=n_heads, dec_seq=dec_seq)
    grid_spec = pltpu.PrefetchScalarGridSpec(
        num_scalar_prefetch=1,
        grid=(nb, n_pages // pages),
        in_specs=(
            [pl.BlockSpec((1, dec_seq, c), per_b),
             pl.BlockSpec((1, SUBLANES, c), per_b),
             pl.BlockSpec((1, SUBLANES, c), per_b),
             pl.BlockSpec((1, n_heads, LANES), per_b),
             pl.BlockSpec((1, c), lambda b, g, pt: (0, 0))]
            + [pl.BlockSpec((1, PAGE_SIZE, c), page(r)) for r in range(pages)]
            + [pl.BlockSpec((1, PAGE_SIZE, c), page(r)) for r in range(pages)]
            + [pl.BlockSpec((1, n_heads, LANES), page(r)) for r in range(pages)]
        ),
        out_specs=pl.BlockSpec((1, dec_seq, c), per_b),
        scratch_shapes=[
            pltpu.VMEM((rows, c), BF16),
            pltpu.VMEM((pages * PAGE_SIZE, c), BF16),
            pltpu.VMEM((pages * PAGE_SIZE, c), BF16),
            pltpu.VMEM((rows, 1), F32),
            pltpu.VMEM((rows, 1), F32),
            pltpu.VMEM((rows, c), F32),
            pltpu.VMEM((n_heads, LANES), F32),
        ],
    )
    return pl.pallas_call(
        kern,
        grid_spec=grid_spec,
        out_shape=jax.ShapeDtypeStruct((nb, dec_seq, c), F32),
        compiler_params=_params("arbitrary", "arbitrary"),
        name="fox_sample",
    )(page_table, q, k_new, v_new, f_new, ga,
      *([cache_k] * pages), *([cache_v] * pages), *([pfx] * pages))


def _mixout_kernel(h_ref, zc_ref, za_ref, wc_ref, wa_ref, o_ref):
    o_ref[...] = (h_ref[...] + _dot(zc_ref[...].astype(BF16), wc_ref[...])
                  + _dot(za_ref[...].astype(BF16), wa_ref[...]))


def _mixout(h, zc, za, wc, wa, tm):
    t, d = h.shape
    c = zc.shape[1]
    row = lambda i: (i, 0)
    fixed = lambda i: (0, 0)
    return pl.pallas_call(
        _mixout_kernel,
        grid=(t // tm,),
        in_specs=[pl.BlockSpec((tm, d), row), pl.BlockSpec((tm, c), row), pl.BlockSpec((tm, c), row),
                  pl.BlockSpec((c, d), fixed), pl.BlockSpec((za.shape[1], d), fixed)],
        out_specs=pl.BlockSpec((tm, d), row),
        out_shape=jax.ShapeDtypeStruct((t, d), F32),
        compiler_params=_params("parallel"),
        name="mixer_out",
    )(h, zc, za, wc, wa)


def _ple_kernel(h_ref, p_ref, g_ref, wg_ref, wp_ref, gf_ref, o_ref):
    h = h_ref[...]
    hn = _rms(h, g_ref[...]).astype(BF16)
    gate = jax.nn.sigmoid(_dot(hn, wg_ref[...]))
    proj = _dot(p_ref[...].astype(BF16), wp_ref[...])
    o_ref[...] = _rms(h + gate * proj, gf_ref[...])


def _ple_final(h, p, g, wg, wp, gf, tm):
    t, d = h.shape
    pd = p.shape[1]
    row = lambda i: (i, 0)
    fixed = lambda i: (0, 0)
    return pl.pallas_call(
        _ple_kernel,
        grid=(t // tm,),
        in_specs=[pl.BlockSpec((tm, d), row), pl.BlockSpec((tm, pd), row), pl.BlockSpec((1, d), fixed),
                  pl.BlockSpec((d, d), fixed), pl.BlockSpec((pd, d), fixed), pl.BlockSpec((1, d), fixed)],
        out_specs=pl.BlockSpec((tm, d), row),
        out_shape=jax.ShapeDtypeStruct((t, d), F32),
        compiler_params=_params("parallel"),
        name="ple_final_norm",
    )(h, p, g, wg, wp, gf)


def _ple_kernel_nofinal(h_ref, p_ref, g_ref, wg_ref, wp_ref, o_ref):
    h = h_ref[...]
    hn = _rms(h, g_ref[...]).astype(BF16)
    gate = jax.nn.sigmoid(_dot(hn, wg_ref[...]))
    o_ref[...] = h + gate * _dot(p_ref[...].astype(BF16), wp_ref[...])


def _ple(h, p, g, wg, wp, tm):
    t, d = h.shape
    pd = p.shape[1]
    row = lambda i: (i, 0)
    fixed = lambda i: (0, 0)
    return pl.pallas_call(
        _ple_kernel_nofinal,
        grid=(t // tm,),
        in_specs=[pl.BlockSpec((tm, d), row), pl.BlockSpec((tm, pd), row), pl.BlockSpec((1, d), fixed),
                  pl.BlockSpec((d, d), fixed), pl.BlockSpec((pd, d), fixed)],
        out_specs=pl.BlockSpec((tm, d), row),
        out_shape=jax.ShapeDtypeStruct((t, d), F32),
        compiler_params=_params("parallel"),
        name="ple",
    )(h, p, g, wg, wp)


def _final_norm_kernel(h_ref, g_ref, o_ref):
    o_ref[...] = _rms(h_ref[...], g_ref[...])


def _final_norm(h, g, tm):
    t, d = h.shape
    return pl.pallas_call(
        _final_norm_kernel,
        grid=(t // tm,),
        in_specs=[pl.BlockSpec((tm, d), lambda i: (i, 0)), pl.BlockSpec((1, d), lambda i: (0, 0))],
        out_specs=pl.BlockSpec((tm, d), lambda i: (i, 0)),
        out_shape=jax.ShapeDtypeStruct((t, d), F32),
        compiler_params=_params("parallel"),
        name="final_norm",
    )(h, g)


def kernel(x_prompt, x_sample, p_prompt, p_sample, cache_k, cache_v, cache_logf, state_conv, page_table, norm_ffn1, w_ffn1_gate, w_ffn1_up, w_ffn1_down, norm_mix, w_in, b_f, conv_w, norm_conv_out, norm_attn_out, w_out, norm_ffn2, w_ffn2_gate, w_ffn2_up, w_ffn2_down, norm_ple, w_ple_gate, w_ple_proj, norm_final):
    batch, seq, d = x_prompt.shape
    dec_batch, dec_seq, _ = x_sample.shape
    depth = w_in.shape[0]
    n_heads = b_f.shape[1]
    conv_dim = conv_w.shape[2]
    attn_dim = n_heads * HEAD_DIM
    n_pool = cache_k.shape[1]
    scale = HEAD_DIM ** -0.5
    n_main = 3 * conv_dim + 3 * attn_dim
    assert conv_dim == attn_dim and dec_seq >= CONV_W - 1 and dec_seq <= SUBLANES
    tp, ts = batch * seq, dec_batch * dec_seq

    hp = x_prompt.reshape(tp, d)
    hs = x_sample.reshape(ts, d)
    row2 = lambda a: a.reshape(1, -1)
    page_prefix_w = _prefix_matrix(PAGE_SIZE, n_heads, PAGE_SIZE)
    new_prefix_w = _prefix_matrix(dec_seq, n_heads, LANES)

    outs = [[] for _ in range(8)]
    for l in range(depth):
        wg1, wu1, wd1 = (w.astype(BF16) for w in (w_ffn1_gate[l], w_ffn1_up[l], w_ffn1_down[l]))
        wg2, wu2, wd2 = (w.astype(BF16) for w in (w_ffn2_gate[l], w_ffn2_up[l], w_ffn2_down[l]))
        w_main = w_in[l, :, :n_main].astype(BF16)
        w_f = jnp.pad(w_in[l, :, n_main:], ((0, 0), (0, LANES - n_heads))).astype(BF16)
        w_oc = w_out[l, :conv_dim].astype(BF16)
        w_oa = w_out[l, conv_dim:].astype(BF16)
        w_pg = w_ple_gate[l].astype(BF16)
        w_pp = w_ple_proj[l].astype(BF16)
        g1, gm, g2, gp = row2(norm_ffn1[l]), row2(norm_mix[l]), row2(norm_ffn2[l]), row2(norm_ple[l])
        gc, ga, bf = row2(norm_conv_out[l]), row2(norm_attn_out[l]), row2(b_f[l])
        cw = conv_w[l]
        last = l == depth - 1
        gfin = row2(norm_final)

        hp = _ffn(hp, g1, wg1, wu1, wd1, TOKEN_TILE)
        zc, q, k, v, kb, vb, lf, cs = _mixin_prompt(hp, gm, w_main, w_f, bf, cw, gc, seq, n_heads, scale)
        lft = lf.reshape(batch, seq, n_heads).transpose(0, 2, 1)
        za = _attn_prompt(q, kb, vb, lft, ga, n_heads)
        hp = _mixout(hp, zc, za, w_oc, w_oa, TOKEN_TILE)
        hp = _ffn(hp, g2, wg2, wu2, wd2, TOKEN_TILE)
        pp = p_prompt[l].reshape(tp, -1)
        hp = (_ple_final(hp, pp, gp, w_pg, w_pp, gfin, TOKEN_TILE) if last
              else _ple(hp, pp, gp, w_pg, w_pp, TOKEN_TILE))
        outs[0].append(k.reshape(batch, seq, n_heads, HEAD_DIM))
        outs[1].append(v.reshape(batch, seq, n_heads, HEAD_DIM))
        outs[2].append(lf.reshape(batch, seq, n_heads))
        outs[3].append(cs)

        hs = _ffn(hs, g1, wg1, wu1, wd1, ts)
        st = state_conv[l]
        zero = jnp.zeros((dec_batch, dec_seq - 1, conv_dim), F32)
        s1 = jnp.concatenate([st[:, 1:2], zero], axis=1).reshape(ts, conv_dim)
        s2 = jnp.concatenate([st, zero[:, 1:]], axis=1).reshape(ts, conv_dim)
        zc, q, k, v, u, lf = _mixin_sample(hs, gm, w_main, w_f, bf, cw, gc, s1, s2,
                                           dec_seq, n_heads, scale)
        pfx = _select_sum(cache_logf[l].reshape(n_pool, PAGE_SIZE * n_heads), page_prefix_w, 512)
        pfx = pfx.reshape(n_pool, n_heads, PAGE_SIZE)
        f_new = _select_sum(lf.reshape(dec_batch, dec_seq * n_heads), new_prefix_w, dec_batch)
        f_new = f_new.reshape(dec_batch, n_heads, LANES)
        pad8 = lambda a: jnp.pad(a.reshape(dec_batch, dec_seq, attn_dim),
                                 ((0, 0), (0, SUBLANES - dec_seq), (0, 0)))
        za = _attn_sample(page_table, q.reshape(dec_batch, dec_seq, attn_dim), pad8(k), pad8(v),
                          f_new, ga, cache_k[l].reshape(n_pool, PAGE_SIZE, attn_dim),
                          cache_v[l].reshape(n_pool, PAGE_SIZE, attn_dim), pfx, n_heads, dec_seq)
        hs = _mixout(hs, zc, za.reshape(ts, attn_dim), w_oc, w_oa, ts)
        hs = _ffn(hs, g2, wg2, wu2, wd2, ts)
        ps = p_sample[l].reshape(ts, -1)
        hs = (_ple_final(hs, ps, gp, w_pg, w_pp, gfin, ts) if last
              else _ple(hs, ps, gp, w_pg, w_pp, ts))
        outs[4].append(k.reshape(dec_batch, dec_seq, n_heads, HEAD_DIM))
        outs[5].append(v.reshape(dec_batch, dec_seq, n_heads, HEAD_DIM))
        outs[6].append(lf.reshape(dec_batch, dec_seq, n_heads))
        outs[7].append(u.reshape(dec_batch, dec_seq, conv_dim)[:, dec_seq - (CONV_W - 1):])

    if depth == 0:
        hp = _final_norm(hp, row2(norm_final), TOKEN_TILE)
        hs = _final_norm(hs, row2(norm_final), ts)
    return (hp.reshape(batch, seq, d), hs.reshape(dec_batch, dec_seq, d),
            *(jnp.stack(o) for o in outs))
```

```python
import functools

import jax
import jax.numpy as jnp
from jax import lax
from jax.experimental import pallas as pl
from jax.experimental.pallas import tpu as pltpu

F32 = jnp.float32
BF16 = jnp.bfloat16

EPS = 1e-6
HEAD_DIM = 128
PAGE_SIZE = 128
CONV_W = 3
LANES = 128
SUBLANES = 8
MASK_VALUE = -1e30
VMEM_LIMIT = 56 * 1024 * 1024

TOKEN_TILE = 512
FF_TILE = 512
ATTN_BLOCK = 256
HEAD_GROUP = 8
LOG2E = 1.4426950408889634
PAGES_PER_STEP = 8


def _params(*sem):
    return pltpu.CompilerParams(dimension_semantics=sem, vmem_limit_bytes=VMEM_LIMIT)


def _rms(x, g):
    ms = jnp.mean(x * x, axis=-1, keepdims=True)
    return x * lax.rsqrt(ms + EPS) * g


def _dot(a, b):
    return jnp.dot(a, b, preferred_element_type=F32)


def _dot_nt(a, b):
    return lax.dot_general(a, b, (((1,), (1,)), ((), ())), preferred_element_type=F32)


def _log_sigmoid(x):
    return jnp.minimum(x, 0.0) - jnp.log1p(jnp.exp(-jnp.abs(x)))


def _ffn_kernel(x_ref, g_ref, wg_ref, wu_ref, wd_ref, o_ref, hn_ref, acc_ref):
    j = pl.program_id(1)

    @pl.when(j == 0)
    def _():
        hn_ref[...] = _rms(x_ref[...], g_ref[...]).astype(BF16)
        acc_ref[...] = jnp.zeros_like(acc_ref)

    hn = hn_ref[...]
    gate = _dot(hn, wg_ref[...])
    up = _dot(hn, wu_ref[...])
    act = (gate * jax.nn.sigmoid(gate) * up).astype(BF16)
    acc_ref[...] += _dot(act, wd_ref[...])

    @pl.when(j == pl.num_programs(1) - 1)
    def _():
        o_ref[...] = x_ref[...] + 0.5 * acc_ref[...]


def _ffn(x, g, wg, wu, wd, tm):
    t, d = x.shape
    f = wg.shape[1]
    tf = FF_TILE
    return pl.pallas_call(
        _ffn_kernel,
        grid=(t // tm, f // tf),
        in_specs=[
            pl.BlockSpec((tm, d), lambda i, j: (i, 0)),
            pl.BlockSpec((1, d), lambda i, j: (0, 0)),
            pl.BlockSpec((d, tf), lambda i, j: (0, j)),
            pl.BlockSpec((d, tf), lambda i, j: (0, j)),
            pl.BlockSpec((tf, d), lambda i, j: (j, 0)),
        ],
        out_specs=pl.BlockSpec((tm, d), lambda i, j: (i, 0)),
        out_shape=jax.ShapeDtypeStruct((t, d), F32),
        scratch_shapes=[pltpu.VMEM((tm, d), BF16), pltpu.VMEM((tm, d), F32)],
        compiler_params=_params("parallel", "arbitrary"),
        name="ffn_half",
    )(x, g, wg, wu, wd)


def _mixin_common(j, x_ref, g_ref, w_ref, wf_ref, bf_ref, lf_ref, lfrep_ref, hn_ref, n_heads):
    @pl.when(j == 0)
    def _():
        hn = _rms(x_ref[...], g_ref[...]).astype(BF16)
        hn_ref[...] = hn
        lf = _log_sigmoid(_dot(hn, wf_ref[...]) + bf_ref[...])
        lf_ref[...] = lf[:, :n_heads]
        if lfrep_ref is not None:
            lfrep_ref[...] = lf

    return lambda: _dot(hn_ref[...], w_ref[...])


def _conv_taps(u, uext_ref, tm):
    uext_ref[pl.ds(SUBLANES, tm), :] = u
    um1 = uext_ref[pl.ds(SUBLANES - 1, tm), :]
    um2 = uext_ref[pl.ds(SUBLANES - 2, tm), :]
    return um1, um2


def _mixin_prompt_kernel(x_ref, g_ref, w_ref, wf_ref, bf_ref, cw_ref, gc_ref,
                         zc_ref, q_ref, k_ref, v_ref, kb_ref, vb_ref, lf_ref, lfrep_ref, cs_ref,
                         hn_ref, cb_ref, cc_ref, uext_ref, *, tm, tiles_per_seq, n_heads, scale):
    i = pl.program_id(0)
    j = pl.program_id(1)
    proj = _mixin_common(j, x_ref, g_ref, w_ref, wf_ref, bf_ref, lf_ref, lfrep_ref, hn_ref, n_heads)

    @pl.when(j == 0)
    def _():
        cb_ref[...] = proj()

    @pl.when(j == 1)
    def _():
        cc_ref[...] = proj()

    @pl.when(j == 2)
    def _():
        @pl.when(i % tiles_per_seq == 0)
        def _():
            uext_ref[pl.ds(0, SUBLANES), :] = jnp.zeros((SUBLANES, cc_ref.shape[1]), F32)

        u = cc_ref[...] * proj()
        um1, um2 = _conv_taps(u, uext_ref, tm)
        y = cb_ref[...] * (cw_ref[0:1, :] * um2 + cw_ref[1:2, :] * um1 + cw_ref[2:3, :] * u)
        zc_ref[...] = _rms(y, gc_ref[...]).astype(BF16)
        cs_ref[0] = u[tm - (CONV_W - 1):, :]
        uext_ref[pl.ds(0, SUBLANES), :] = u[tm - SUBLANES:, :]

    @pl.when(j == 3)
    def _():
        q_ref[...] = (proj() * scale).astype(BF16)

    @pl.when(j == 4)
    def _():
        z = proj()
        k_ref[...] = z
        kb_ref[...] = z.astype(BF16)

    @pl.when(j == 5)
    def _():
        z = proj()
        v_ref[...] = z
        vb_ref[...] = z.astype(BF16)


def _mixin_prompt(x, g, w_main, w_f, b_f, cw, gc, seq, n_heads, scale):
    t, d = x.shape
    c = w_main.shape[1] // 6
    tm = TOKEN_TILE
    row = lambda i, j: (i, 0)
    fixed = lambda i, j: (0, 0)
    kern = functools.partial(_mixin_prompt_kernel, tm=tm, tiles_per_seq=seq // tm,
                             n_heads=n_heads, scale=scale)
    return pl.pallas_call(
        kern,
        grid=(t // tm, 6),
        in_specs=[
            pl.BlockSpec((tm, d), row),
            pl.BlockSpec((1, d), fixed),
            pl.BlockSpec((d, c), lambda i, j: (0, j)),
            pl.BlockSpec((d, LANES), fixed),
            pl.BlockSpec((1, LANES), fixed),
            pl.BlockSpec((CONV_W, c), fixed),
            pl.BlockSpec((1, c), fixed),
        ],
        out_specs=[
            pl.BlockSpec((tm, c), row),
            pl.BlockSpec((tm, c), row),
            pl.BlockSpec((tm, c), row),
            pl.BlockSpec((tm, c), row),
            pl.BlockSpec((tm, c), row),
            pl.BlockSpec((tm, c), row),
            pl.BlockSpec((tm, n_heads), row),
            pl.BlockSpec((tm, LANES), row),
            pl.BlockSpec((1, CONV_W - 1, c), lambda i, j: (i // (seq // tm), 0, 0)),
        ],
        out_shape=[
            jax.ShapeDtypeStruct((t, c), BF16),
            jax.ShapeDtypeStruct((t, c), BF16),
            jax.ShapeDtypeStruct((t, c), F32),
            jax.ShapeDtypeStruct((t, c), F32),
            jax.ShapeDtypeStruct((t, c), BF16),
            jax.ShapeDtypeStruct((t, c), BF16),
            jax.ShapeDtypeStruct((t, n_heads), F32),
            jax.ShapeDtypeStruct((t, LANES), F32),
            jax.ShapeDtypeStruct((t // seq, CONV_W - 1, c), F32),
        ],
        scratch_shapes=[
            pltpu.VMEM((tm, d), BF16),
            pltpu.VMEM((tm, c), F32),
            pltpu.VMEM((tm, c), F32),
            pltpu.VMEM((tm + SUBLANES, c), F32),
        ],
        compiler_params=_params("arbitrary", "arbitrary"),
        name="mixer_in_prompt",
    )(x, g, w_main, w_f, b_f, cw, gc)


def _mixin_sample_kernel(x_ref, g_ref, w_ref, wf_ref, bf_ref, cw_ref, gc_ref, s1_ref, s2_ref,
                         zc_ref, q_ref, k_ref, v_ref, u_ref, lf_ref,
                         hn_ref, cb_ref, cc_ref, uext_ref, *, tm, dec_seq, n_heads, scale):
    j = pl.program_id(1)
    proj = _mixin_common(j, x_ref, g_ref, w_ref, wf_ref, bf_ref, lf_ref, None, hn_ref, n_heads)

    @pl.when(j == 0)
    def _():
        cb_ref[...] = proj()

    @pl.when(j == 1)
    def _():
        cc_ref[...] = proj()

    @pl.when(j == 2)
    def _():
        uext_ref[pl.ds(0, SUBLANES), :] = jnp.zeros((SUBLANES, cc_ref.shape[1]), F32)
        u = cc_ref[...] * proj()
        um1, um2 = _conv_taps(u, uext_ref, tm)
        step = lax.broadcasted_iota(jnp.int32, u.shape, 0) % dec_seq
        um1 = jnp.where(step >= 1, um1, s1_ref[...])
        um2 = jnp.where(step >= 2, um2, s2_ref[...])
        y = cb_ref[...] * (cw_ref[0:1, :] * um2 + cw_ref[1:2, :] * um1 + cw_ref[2:3, :] * u)
        zc_ref[...] = _rms(y, gc_ref[...]).astype(BF16)
        u_ref[...] = u

    @pl.when(j == 3)
    def _():
        q_ref[...] = proj() * scale

    @pl.when(j == 4)
    def _():
        k_ref[...] = proj()

    @pl.when(j == 5)
    def _():
        v_ref[...] = proj()


def _mixin_sample(x, g, w_main, w_f, b_f, cw, gc, s1, s2, dec_seq, n_heads, scale):
    t, d = x.shape
    c = w_main.shape[1] // 6
    tm = t
    row = lambda i, j: (i, 0)
    fixed = lambda i, j: (0, 0)
    kern = functools.partial(_mixin_sample_kernel, tm=tm, dec_seq=dec_seq, n_heads=n_heads,
                             scale=scale)
    return pl.pallas_call(
        kern,
        grid=(1, 6),
        in_specs=[
            pl.BlockSpec((tm, d), row),
            pl.BlockSpec((1, d), fixed),
            pl.BlockSpec((d, c), lambda i, j: (0, j)),
            pl.BlockSpec((d, LANES), fixed),
            pl.BlockSpec((1, LANES), fixed),
            pl.BlockSpec((CONV_W, c), fixed),
            pl.BlockSpec((1, c), fixed),
            pl.BlockSpec((tm, c), row),
            pl.BlockSpec((tm, c), row),
        ],
        out_specs=[
            pl.BlockSpec((tm, c), row),
            pl.BlockSpec((tm, c), row),
            pl.BlockSpec((tm, c), row),
            pl.BlockSpec((tm, c), row),
            pl.BlockSpec((tm, c), row),
            pl.BlockSpec((tm, n_heads), row),
        ],
        out_shape=[
            jax.ShapeDtypeStruct((t, c), BF16),
            jax.ShapeDtypeStruct((t, c), F32),
            jax.ShapeDtypeStruct((t, c), F32),
            jax.ShapeDtypeStruct((t, c), F32),
            jax.ShapeDtypeStruct((t, c), F32),
            jax.ShapeDtypeStruct((t, n_heads), F32),
        ],
        scratch_shapes=[
            pltpu.VMEM((tm, d), BF16),
            pltpu.VMEM((tm, c), F32),
            pltpu.VMEM((tm, c), F32),
            pltpu.VMEM((tm + SUBLANES, c), F32),
        ],
        compiler_params=_params("arbitrary", "arbitrary"),
        name="mixer_in_sample",
    )(x, g, w_main, w_f, b_f, cw, gc, s1, s2)


def _split3(x):
    hi = x.astype(BF16)
    r1 = x - hi.astype(F32)
    mid = r1.astype(BF16)
    lo = (r1 - mid.astype(F32)).astype(BF16)
    return hi, mid, lo


def _attn_prompt_kernel(q_ref, k_ref, v_ref, lfrep_ref, ga_ref, za_ref, aug_ref, o_ref,
                        *, blk, n_heads):
    i = pl.program_id(1)
    seq = k_ref.shape[0]
    lane = lax.broadcasted_iota(jnp.int32, (blk, LANES), 1)
    row = lax.broadcasted_iota(jnp.int32, (blk, blk), 0)
    col = lax.broadcasted_iota(jnp.int32, (blk, blk), 1)
    causal = col <= row

    @pl.when(i == 0)
    def _():
        tri = jnp.where(causal, 1.0, 0.0).astype(BF16)
        carry = jnp.zeros((1, LANES), F32)
        for jb in range(seq // blk):
            rows = slice(jb * blk, (jb + 1) * blk)
            hi, mid, lo = _split3(lfrep_ref[rows, :])
            f = _dot(tri, hi) + _dot(tri, mid) + _dot(tri, lo) + carry
            carry = f[blk - 1:blk, :]
            hi, mid, lo = _split3(f * (-LOG2E))
            zero = jnp.zeros_like(hi)
            aug_ref[rows, :] = jnp.where(
                lane < n_heads, hi,
                jnp.where(lane < 2 * n_heads, mid, jnp.where(lane < 3 * n_heads, lo, zero)))

    def step(jb, carries, heads, q_aug, masked):
        start = pl.multiple_of(jb * blk, blk)
        aj = aug_ref[pl.ds(start, blk), :]
        out = []
        for h, qa, (m, l, acc) in zip(heads, q_aug, carries):
            hs = slice(h * HEAD_DIM, (h + 1) * HEAD_DIM)
            kj = jnp.concatenate([k_ref[pl.ds(start, blk), hs], aj], axis=1)
            s = _dot_nt(qa, kj)
            if masked:
                s = jnp.where(causal, s, MASK_VALUE)
            m_new = jnp.maximum(m, jnp.max(s, axis=-1, keepdims=True))
            alpha = jnp.exp2(m - m_new)
            p = jnp.exp2(s - m_new)
            l = alpha * l + jnp.sum(p, axis=-1, keepdims=True)
            acc = alpha * acc + _dot(p.astype(BF16), v_ref[pl.ds(start, blk), hs])
            out.append((m_new, l, acc))
        return tuple(out)

    for h0 in range(0, n_heads, HEAD_GROUP):
        heads = tuple(range(h0, h0 + HEAD_GROUP))
        q_aug = []
        for h in heads:
            ones = jnp.where((lane % n_heads == h) & (lane < 3 * n_heads), 1.0, 0.0).astype(BF16)
            q_aug.append(jnp.concatenate([q_ref[:, h * HEAD_DIM:(h + 1) * HEAD_DIM], ones], axis=1))
        init = tuple((jnp.full((blk, 1), MASK_VALUE, F32), jnp.zeros((blk, 1), F32),
                      jnp.zeros((blk, HEAD_DIM), F32)) for _ in heads)
        carries = lax.fori_loop(0, i, lambda jb, c: step(jb, c, heads, q_aug, False), init)
        carries = step(i, carries, heads, q_aug, True)
        for h, (m, l, acc) in zip(heads, carries):
            o_ref[:, h * HEAD_DIM:(h + 1) * HEAD_DIM] = acc / l

    za_ref[...] = _rms(o_ref[...], ga_ref[...]).astype(BF16)


def _attn_prompt(q, kb, vb, lfrep, ga, seq, n_heads):
    t, c = q.shape
    b = t // seq
    blk = ATTN_BLOCK
    nq = seq // blk
    kern = functools.partial(_attn_prompt_kernel, blk=blk, n_heads=n_heads)
    return pl.pallas_call(
        kern,
        grid=(b, nq),
        in_specs=[
            pl.BlockSpec((blk, c), lambda bi, i: (bi * nq + i, 0)),
            pl.BlockSpec((seq, c), lambda bi, i: (bi, 0)),
            pl.BlockSpec((seq, c), lambda bi, i: (bi, 0)),
            pl.BlockSpec((seq, LANES), lambda bi, i: (bi, 0)),
            pl.BlockSpec((1, c), lambda bi, i: (0, 0)),
        ],
        out_specs=pl.BlockSpec((blk, c), lambda bi, i: (bi * nq + i, 0)),
        out_shape=jax.ShapeDtypeStruct((t, c), BF16),
        scratch_shapes=[pltpu.VMEM((seq, LANES), BF16), pltpu.VMEM((blk, c), F32)],
        compiler_params=_params("arbitrary", "arbitrary"),
        name="fox_prompt",
    )(q, kb, vb, lfrep, ga)


def _split3_dot(x, w):
    hi, mid, lo = _split3(x)
    return _dot(hi, w) + _dot(mid, w) + _dot(lo, w)


def _select_sum_kernel(x_ref, w_ref, o_ref):
    o_ref[...] = _split3_dot(x_ref[...], w_ref[...])


def _select_sum(x, w, tm):
    rows, k = x.shape
    n = w.shape[1]
    return pl.pallas_call(
        _select_sum_kernel,
        grid=(rows // tm,),
        in_specs=[pl.BlockSpec((tm, k), lambda i: (i, 0)), pl.BlockSpec((k, n), lambda i: (0, 0))],
        out_specs=pl.BlockSpec((tm, n), lambda i: (i, 0)),
        out_shape=jax.ShapeDtypeStruct((rows, n), F32),
        compiler_params=_params("parallel"),
        name="head_major_prefix_sum",
    )(x, w)


def _prefix_matrix(n_steps, n_heads, width):
    r = jnp.arange(n_steps * n_heads)
    cidx = jnp.arange(n_heads * width)
    t, h = r // n_heads, r % n_heads
    h2, t2 = cidx // width, cidx % width
    w = (h[:, None] == h2[None, :]) & (t[:, None] <= t2[None, :]) & (t2[None, :] < n_steps)
    return w.astype(BF16)


def _attn_sample_kernel(pt_ref, q_ref, kn_ref, vn_ref, fn_ref, ga_ref, *rest,
                        pages, n_heads, dec_seq):
    k_refs = rest[:pages]
    v_refs = rest[pages:2 * pages]
    pf_refs = rest[2 * pages:3 * pages]
    za_ref = rest[3 * pages]
    qbd_ref, kb_ref, vb_ref, m_ref, l_ref, acc_ref, run_ref = rest[3 * pages + 1:]
    g = pl.program_id(1)
    rows = dec_seq * n_heads
    c = n_heads * HEAD_DIM
    head_of_row = lax.broadcasted_iota(jnp.int32, (n_heads, c), 0)
    head_of_col = lax.broadcasted_iota(jnp.int32, (n_heads, c), 1) // HEAD_DIM
    diag = head_of_row == head_of_col

    @pl.when(g == 0)
    def _():
        for t in range(dec_seq):
            qt = jnp.broadcast_to(q_ref[0, t:t + 1, :], (n_heads, c))
            qbd_ref[t * n_heads:(t + 1) * n_heads, :] = jnp.where(diag, qt, 0.0).astype(BF16)
        m_ref[...] = jnp.full(m_ref.shape, MASK_VALUE, F32)
        l_ref[...] = jnp.zeros(l_ref.shape, F32)
        acc_ref[...] = jnp.zeros(acc_ref.shape, F32)
        run_ref[...] = jnp.zeros(run_ref.shape, F32)

    def online_update(s, v_bf):
        m_old = m_ref[...]
        m_new = jnp.maximum(m_old, jnp.max(s, axis=-1, keepdims=True))
        alpha = jnp.exp(m_old - m_new)
        p = jnp.exp(s - m_new)
        l_ref[...] = alpha * l_ref[...] + jnp.sum(p, axis=-1, keepdims=True)
        acc_ref[...] = alpha * acc_ref[...] + _dot(p.astype(BF16), v_bf)
        m_ref[...] = m_new

    run = run_ref[...]
    bias_parts = []
    for pg in range(pages):
        toks = slice(pg * PAGE_SIZE, (pg + 1) * PAGE_SIZE)
        for h in range(n_heads):
            hs = slice(h * HEAD_DIM, (h + 1) * HEAD_DIM)
            head_rows = pl.ds(h, PAGE_SIZE, stride=n_heads)
            kb_ref[toks, hs] = k_refs[pg][0, 0, head_rows, :].astype(BF16)
            vb_ref[toks, hs] = v_refs[pg][0, 0, head_rows, :].astype(BF16)
        pf = pf_refs[pg][0]
        bias_parts.append(run + pf)
        run = run + jnp.broadcast_to(pf[:, PAGE_SIZE - 1:PAGE_SIZE], pf.shape)
    run_ref[...] = run
    bias = jnp.concatenate(bias_parts, axis=1)
    bias = jnp.concatenate([bias] * dec_seq, axis=0)
    s = _dot_nt(qbd_ref[...], kb_ref[...]) - bias
    online_update(s, vb_ref[...])

    @pl.when(g == pl.num_programs(1) - 1)
    def _():
        pad = jnp.zeros((PAGE_SIZE - SUBLANES, c), F32)
        kn = jnp.concatenate([kn_ref[0], pad], axis=0).astype(BF16)
        vn = jnp.concatenate([vn_ref[0], pad], axis=0).astype(BF16)
        bias_n = jnp.concatenate([run + fn_ref[0]] * dec_seq, axis=0)
        s_n = _dot_nt(qbd_ref[...], kn) - bias_n
        q_step = lax.broadcasted_iota(jnp.int32, (rows, PAGE_SIZE), 0) // n_heads
        k_step = lax.broadcasted_iota(jnp.int32, (rows, PAGE_SIZE), 1)
        s_n = jnp.where(k_step <= q_step, s_n, MASK_VALUE)
        online_update(s_n, vn)
        o = acc_ref[...] / l_ref[...]
        for t in range(dec_seq):
            ot = jnp.where(diag, o[t * n_heads:(t + 1) * n_heads, :], 0.0)
            ot = jnp.sum(ot, axis=0, keepdims=True)
            za_ref[0, t:t + 1, :] = _rms(ot, ga_ref[...])


def _attn_sample(page_table, q, k_new, v_new, f_new, ga, cache_k, cache_v, pfx, layer, n_heads,
                 dec_seq):
    nb, n_pages = page_table.shape
    c = n_heads * HEAD_DIM
    pages = PAGES_PER_STEP
    rows = dec_seq * n_heads
    per_b = lambda b, g, pt: (b, 0, 0)
    page = lambda r: (lambda b, g, pt: (pt[b, g * pages + r], 0, 0))
    kv_page = lambda r: (lambda b, g, pt: (layer, pt[b, g * pages + r], 0, 0))
    kv_block = (1, 1, PAGE_SIZE * n_heads, HEAD_DIM)
    kern = functools.partial(_attn_sample_kernel, pages=pages, n_heads=n_heads, dec_seq=dec_seq)
    grid_spec = pltpu.PrefetchScalarGridSpec(
        num_scalar_prefetch=1,
        grid=(nb, n_pages // pages),
        in_specs=(
            [pl.BlockSpec((1, dec_seq, c), per_b),
             pl.BlockSpec((1, SUBLANES, c), per_b),
             pl.BlockSpec((1, SUBLANES, c), per_b),
             pl.BlockSpec((1, n_heads, LANES), per_b),
             pl.BlockSpec((1, c), lambda b, g, pt: (0, 0))]
            + [pl.BlockSpec(kv_block, kv_page(r)) for r in range(pages)]
            + [pl.BlockSpec(kv_block, kv_page(r)) for r in range(pages)]
            + [pl.BlockSpec((1, n_heads, LANES), page(r)) for r in range(pages)]
        ),
        out_specs=pl.BlockSpec((1, dec_seq, c), per_b),
        scratch_shapes=[
            pltpu.VMEM((rows, c), BF16),
            pltpu.VMEM((pages * PAGE_SIZE, c), BF16),
            pltpu.VMEM((pages * PAGE_SIZE, c), BF16),
            pltpu.VMEM((rows, 1), F32),
            pltpu.VMEM((rows, 1), F32),
            pltpu.VMEM((rows, c), F32),
            pltpu.VMEM((n_heads, LANES), F32),
        ],
    )
    return pl.pallas_call(
        kern,
        grid_spec=grid_spec,
        out_shape=jax.ShapeDtypeStruct((nb, dec_seq, c), F32),
        compiler_params=_params("arbitrary", "arbitrary"),
        name="fox_sample",
    )(page_table, q, k_new, v_new, f_new, ga,
      *([cache_k] * pages), *([cache_v] * pages), *([pfx] * pages))


def _mixout_kernel(h_ref, zc_ref, za_ref, wc_ref, wa_ref, o_ref):
    o_ref[...] = (h_ref[...] + _dot(zc_ref[...].astype(BF16), wc_ref[...])
                  + _dot(za_ref[...].astype(BF16), wa_ref[...]))


def _mixout(h, zc, za, wc, wa, tm):
    t, d = h.shape
    c = zc.shape[1]
    row = lambda i: (i, 0)
    fixed = lambda i: (0, 0)
    return pl.pallas_call(
        _mixout_kernel,
        grid=(t // tm,),
        in_specs=[pl.BlockSpec((tm, d), row), pl.BlockSpec((tm, c), row), pl.BlockSpec((tm, c), row),
                  pl.BlockSpec((c, d), fixed), pl.BlockSpec((za.shape[1], d), fixed)],
        out_specs=pl.BlockSpec((tm, d), row),
        out_shape=jax.ShapeDtypeStruct((t, d), F32),
        compiler_params=_params("parallel"),
        name="mixer_out",
    )(h, zc, za, wc, wa)


def _ple_kernel(h_ref, p_ref, g_ref, wg_ref, wp_ref, gf_ref, o_ref):
    h = h_ref[...]
    hn = _rms(h, g_ref[...]).astype(BF16)
    gate = jax.nn.sigmoid(_dot(hn, wg_ref[...]))
    proj = _dot(p_ref[...].astype(BF16), wp_ref[...])
    o_ref[...] = _rms(h + gate * proj, gf_ref[...])


def _ple_final(h, p, g, wg, wp, gf, tm):
    t, d = h.shape
    pd = p.shape[1]
    row = lambda i: (i, 0)
    fixed = lambda i: (0, 0)
    return pl.pallas_call(
        _ple_kernel,
        grid=(t // tm,),
        in_specs=[pl.BlockSpec((tm, d), row), pl.BlockSpec((tm, pd), row), pl.BlockSpec((1, d), fixed),
                  pl.BlockSpec((d, d), fixed), pl.BlockSpec((pd, d), fixed), pl.BlockSpec((1, d), fixed)],
        out_specs=pl.BlockSpec((tm, d), row),
        out_shape=jax.ShapeDtypeStruct((t, d), F32),
        compiler_params=_params("parallel"),
        name="ple_final_norm",
    )(h, p, g, wg, wp, gf)


def _ple_kernel_nofinal(h_ref, p_ref, g_ref, wg_ref, wp_ref, o_ref):
    h = h_ref[...]
    hn = _rms(h, g_ref[...]).astype(BF16)
    gate = jax.nn.sigmoid(_dot(hn, wg_ref[...]))
    o_ref[...] = h + gate * _dot(p_ref[...].astype(BF16), wp_ref[...])


def _ple(h, p, g, wg, wp, tm):
    t, d = h.shape
    pd = p.shape[1]
    row = lambda i: (i, 0)
    fixed = lambda i: (0, 0)
    return pl.pallas_call(
        _ple_kernel_nofinal,
        grid=(t // tm,),
        in_specs=[pl.BlockSpec((tm, d), row), pl.BlockSpec((tm, pd), row), pl.BlockSpec((1, d), fixed),
                  pl.BlockSpec((d, d), fixed), pl.BlockSpec((pd, d), fixed)],
        out_specs=pl.BlockSpec((tm, d), row),
        out_shape=jax.ShapeDtypeStruct((t, d), F32),
        compiler_params=_params("parallel"),
        name="ple",
    )(h, p, g, wg, wp)


def _final_norm_kernel(h_ref, g_ref, o_ref):
    o_ref[...] = _rms(h_ref[...], g_ref[...])


def _final_norm(h, g, tm):
    t, d = h.shape
    return pl.pallas_call(
        _final_norm_kernel,
        grid=(t // tm,),
        in_specs=[pl.BlockSpec((tm, d), lambda i: (i, 0)), pl.BlockSpec((1, d), lambda i: (0, 0))],
        out_specs=pl.BlockSpec((tm, d), lambda i: (i, 0)),
        out_shape=jax.ShapeDtypeStruct((t, d), F32),
        compiler_params=_params("parallel"),
        name="final_norm",
    )(h, g)


def kernel(x_prompt, x_sample, p_prompt, p_sample, cache_k, cache_v, cache_logf, state_conv, page_table, norm_ffn1, w_ffn1_gate, w_ffn1_up, w_ffn1_down, norm_mix, w_in, b_f, conv_w, norm_conv_out, norm_attn_out, w_out, norm_ffn2, w_ffn2_gate, w_ffn2_up, w_ffn2_down, norm_ple, w_ple_gate, w_ple_proj, norm_final):
    batch, seq, d = x_prompt.shape
    dec_batch, dec_seq, _ = x_sample.shape
    depth = w_in.shape[0]
    n_heads = b_f.shape[1]
    conv_dim = conv_w.shape[2]
    attn_dim = n_heads * HEAD_DIM
    n_pool = cache_k.shape[1]
    scale = HEAD_DIM ** -0.5
    n_main = 3 * conv_dim + 3 * attn_dim
    assert conv_dim == attn_dim and dec_seq >= CONV_W - 1 and dec_seq <= SUBLANES
    tp, ts = batch * seq, dec_batch * dec_seq

    hp = x_prompt.reshape(tp, d)
    hs = x_sample.reshape(ts, d)
    row2 = lambda a: a.reshape(1, -1)
    page_prefix_w = _prefix_matrix(PAGE_SIZE, n_heads, PAGE_SIZE)
    new_prefix_w = _prefix_matrix(dec_seq, n_heads, LANES)

    outs = [[] for _ in range(8)]
    for l in range(depth):
        wg1, wu1, wd1 = (w.astype(BF16) for w in (w_ffn1_gate[l], w_ffn1_up[l], w_ffn1_down[l]))
        wg2, wu2, wd2 = (w.astype(BF16) for w in (w_ffn2_gate[l], w_ffn2_up[l], w_ffn2_down[l]))
        w_main = w_in[l, :, :n_main].astype(BF16)
        w_f = jnp.tile(w_in[l, :, n_main:], (1, LANES // n_heads)).astype(BF16)
        w_oc = w_out[l, :conv_dim].astype(BF16)
        w_oa = w_out[l, conv_dim:].astype(BF16)
        w_pg = w_ple_gate[l].astype(BF16)
        w_pp = w_ple_proj[l].astype(BF16)
        g1, gm, g2, gp = row2(norm_ffn1[l]), row2(norm_mix[l]), row2(norm_ffn2[l]), row2(norm_ple[l])
        gc, ga = row2(norm_conv_out[l]), row2(norm_attn_out[l])
        bf = row2(jnp.tile(b_f[l], LANES // n_heads))
        cw = conv_w[l]
        last = l == depth - 1
        gfin = row2(norm_final)

        hp = _ffn(hp, g1, wg1, wu1, wd1, TOKEN_TILE)
        zc, q, k, v, kb, vb, lf, lfrep, cs = _mixin_prompt(hp, gm, w_main, w_f, bf, cw, gc, seq,
                                                           n_heads, scale * LOG2E)
        za = _attn_prompt(q, kb, vb, lfrep, ga, seq, n_heads)
        hp = _mixout(hp, zc, za, w_oc, w_oa, TOKEN_TILE)
        hp = _ffn(hp, g2, wg2, wu2, wd2, TOKEN_TILE)
        pp = p_prompt[l].reshape(tp, -1)
        hp = (_ple_final(hp, pp, gp, w_pg, w_pp, gfin, TOKEN_TILE) if last
              else _ple(hp, pp, gp, w_pg, w_pp, TOKEN_TILE))
        outs[0].append(k.reshape(batch, seq, n_heads, HEAD_DIM))
        outs[1].append(v.reshape(batch, seq, n_heads, HEAD_DIM))
        outs[2].append(lf.reshape(batch, seq, n_heads))
        outs[3].append(cs)

        hs = _ffn(hs, g1, wg1, wu1, wd1, ts)
        st = state_conv[l]
        zero = jnp.zeros((dec_batch, dec_seq - 1, conv_dim), F32)
        s1 = jnp.concatenate([st[:, 1:2], zero], axis=1).reshape(ts, conv_dim)
        s2 = jnp.concatenate([st, zero[:, 1:]], axis=1).reshape(ts, conv_dim)
        zc, q, k, v, u, lf = _mixin_sample(hs, gm, w_main, w_f, bf, cw, gc, s1, s2,
                                           dec_seq, n_heads, scale)
        pfx = _select_sum(cache_logf[l].reshape(n_pool, PAGE_SIZE * n_heads), page_prefix_w, 512)
        pfx = pfx.reshape(n_pool, n_heads, PAGE_SIZE)
        f_new = _select_sum(lf.reshape(dec_batch, dec_seq * n_heads), new_prefix_w, dec_batch)
        f_new = f_new.reshape(dec_batch, n_heads, LANES)
        pad8 = lambda a: jnp.pad(a.reshape(dec_batch, dec_seq, attn_dim),
                                 ((0, 0), (0, SUBLANES - dec_seq), (0, 0)))
        za = _attn_sample(page_table, q.reshape(dec_batch, dec_seq, attn_dim), pad8(k), pad8(v),
                          f_new, ga,
                          cache_k.reshape(depth, n_pool, PAGE_SIZE * n_heads, HEAD_DIM),
                          cache_v.reshape(depth, n_pool, PAGE_SIZE * n_heads, HEAD_DIM),
                          pfx, l, n_heads, dec_seq)
        hs = _mixout(hs, zc, za.reshape(ts, attn_dim), w_oc, w_oa, ts)
        hs = _ffn(hs, g2, wg2, wu2, wd2, ts)
        ps = p_sample[l].reshape(ts, -1)
        hs = (_ple_final(hs, ps, gp, w_pg, w_pp, gfin, ts) if last
              else _ple(hs, ps, gp, w_pg, w_pp, ts))
        outs[4].append(k.reshape(dec_batch, dec_seq, n_heads, HEAD_DIM))
        outs[5].append(v.reshape(dec_batch, dec_seq, n_heads, HEAD_DIM))
        outs[6].append(lf.reshape(dec_batch, dec_seq, n_heads))
        outs[7].append(u.reshape(dec_batch, dec_seq, conv_dim)[:, dec_seq - (CONV_W - 1):])

    return (hp.reshape(batch, seq, d), hs.reshape(dec_batch, dec_seq, d),
            *(jnp.stack(o) for o in outs))
```

```python
import functools

import jax
import jax.numpy as jnp
from jax import lax
from jax.experimental import pallas as pl
from jax.experimental.pallas import tpu as pltpu

F32 = jnp.float32
BF16 = jnp.bfloat16

EPS = 1e-6
HEAD_DIM = 128
PAGE_SIZE = 128
CONV_W = 3
LANES = 128
SUBLANES = 8
MASK_VALUE = -1e30
VMEM_LIMIT = 56 * 1024 * 1024

TOKEN_TILE = 512
FFN_TOKEN_TILE = 1024
FFN_FF_TILE = 256
ATTN_BLOCK = 256
HEAD_GROUP = 8
LOG2E = 1.4426950408889634
PAGES_PER_STEP = 16


def _params(*sem):
    return pltpu.CompilerParams(dimension_semantics=sem, vmem_limit_bytes=VMEM_LIMIT)


def _rms(x, g):
    ms = jnp.mean(x * x, axis=-1, keepdims=True)
    return x * lax.rsqrt(ms + EPS) * g


def _dot(a, b):
    return jnp.dot(a, b, preferred_element_type=F32)


def _dot_nt(a, b):
    return lax.dot_general(a, b, (((1,), (1,)), ((), ())), preferred_element_type=F32)


def _log_sigmoid(x):
    return jnp.minimum(x, 0.0) - jnp.log1p(jnp.exp(-jnp.abs(x)))


def _ffn_kernel(x_hbm, xs_ref, g_ref, wg_ref, wu_ref, wd_ref, o_ref, os_ref,
                xbuf_ref, hn_ref, hns_ref, sem, *, tm):
    i = pl.program_id(0)
    j = pl.program_id(1)
    last_tile = pl.num_programs(0) - 1

    def x_copy(tile):
        return pltpu.make_async_copy(x_hbm.at[pl.ds(tile * tm, tm), :], xbuf_ref, sem)

    @pl.when((i == 0) & (j == 0))
    def _():
        x_copy(0).start()

    @pl.when(j == 0)
    def _():
        x_copy(i).wait()
        x = xbuf_ref[...]
        hn_ref[...] = _rms(x, g_ref[...]).astype(BF16)
        o_ref[...] = x

    @pl.when((j == 1) & (i < last_tile))
    def _():
        x_copy(i + 1).start()

    wg = wg_ref[...].astype(BF16)
    wu = wu_ref[...].astype(BF16)
    wd = wd_ref[...].astype(BF16)

    def half_step(hn):
        gate = _dot(hn, wg)
        up = _dot(hn, wu)
        act = (gate * jax.nn.sigmoid(gate) * (0.5 * up)).astype(BF16)
        return _dot(act, wd)

    o_ref[...] += half_step(hn_ref[...])

    @pl.when(i == last_tile)
    def _():
        @pl.when(j == 0)
        def _():
            xs = xs_ref[...]
            hns_ref[...] = _rms(xs, g_ref[...]).astype(BF16)
            os_ref[...] = xs

        os_ref[...] += half_step(hns_ref[...])


def _ffn(x, xs, g, wg, wu, wd, layer):
    t, d = x.shape
    ts = xs.shape[0]
    f = wg.shape[2]
    tm, tf = FFN_TOKEN_TILE, FFN_FF_TILE
    assert f % tf == 0 and f // tf >= 2 and t % tm == 0
    return pl.pallas_call(
        functools.partial(_ffn_kernel, tm=tm),
        grid=(t // tm, f // tf),
        in_specs=[
            pl.BlockSpec(memory_space=pl.ANY),
            pl.BlockSpec((ts, d), lambda i, j: (0, 0)),
            pl.BlockSpec((1, d), lambda i, j: (0, 0)),
            pl.BlockSpec((None, d, tf), lambda i, j: (layer, 0, j)),
            pl.BlockSpec((None, d, tf), lambda i, j: (layer, 0, j)),
            pl.BlockSpec((None, tf, d), lambda i, j: (layer, j, 0)),
        ],
        out_specs=[pl.BlockSpec((tm, d), lambda i, j: (i, 0)),
                   pl.BlockSpec((ts, d), lambda i, j: (0, 0))],
        out_shape=[jax.ShapeDtypeStruct((t, d), F32), jax.ShapeDtypeStruct((ts, d), F32)],
        scratch_shapes=[pltpu.VMEM((tm, d), F32), pltpu.VMEM((tm, d), BF16),
                        pltpu.VMEM((ts, d), BF16), pltpu.SemaphoreType.DMA(())],
        compiler_params=_params("arbitrary", "arbitrary"),
        name="ffn_half",
    )(x, xs, g, wg, wu, wd)


def _mixin_common(j, x_ref, g_ref, w_ref, wf_ref, bf_ref, lf_ref, lfrep_ref, hn_ref, cb_ref,
                  n_heads):
    @pl.when(j == 0)
    def _():
        hn = _rms(x_ref[...], g_ref[...]).astype(BF16)
        hn_ref[...] = hn
        cb_ref[...] = _dot(hn, w_ref[...])
        lf = _log_sigmoid(_dot(hn, wf_ref[...]) + bf_ref[...])
        lf_ref[...] = lf[:, :n_heads]
        if lfrep_ref is not None:
            lfrep_ref[...] = lf

    return lambda: _dot(hn_ref[...], w_ref[...])


def _conv_taps(u, uext_ref, tm):
    uext_ref[pl.ds(SUBLANES, tm), :] = u
    um1 = uext_ref[pl.ds(SUBLANES - 1, tm), :]
    um2 = uext_ref[pl.ds(SUBLANES - 2, tm), :]
    return um1, um2


def _mixin_prompt_kernel(x_ref, g_ref, w_ref, wf_ref, bf_ref, cw_ref, gc_ref,
                         zc_ref, q_ref, k_ref, v_ref, kb_ref, vb_ref, lf_ref, lfrep_ref, cs_ref,
                         hn_ref, cb_ref, cc_ref, uext_ref, *, tm, tiles_per_seq, n_heads, scale):
    i = pl.program_id(0)
    j = pl.program_id(1)
    proj = _mixin_common(j, x_ref, g_ref, w_ref, wf_ref, bf_ref, lf_ref, lfrep_ref, hn_ref, cb_ref,
                         n_heads)

    @pl.when(j == 1)
    def _():
        cc_ref[...] = proj()

    @pl.when(j == 2)
    def _():
        @pl.when(i % tiles_per_seq == 0)
        def _():
            uext_ref[pl.ds(0, SUBLANES), :] = jnp.zeros((SUBLANES, cc_ref.shape[1]), F32)

        u = cc_ref[...] * proj()
        um1, um2 = _conv_taps(u, uext_ref, tm)
        y = cb_ref[...] * (cw_ref[0:1, :] * um2 + cw_ref[1:2, :] * um1 + cw_ref[2:3, :] * u)
        zc_ref[...] = _rms(y, gc_ref[...]).astype(BF16)
        cs_ref[0] = u[tm - (CONV_W - 1):, :]
        uext_ref[pl.ds(0, SUBLANES), :] = u[tm - SUBLANES:, :]

    @pl.when(j == 3)
    def _():
        q_ref[...] = (proj() * scale).astype(BF16)

    @pl.when(j == 4)
    def _():
        z = proj()
        k_ref[...] = z
        kb_ref[...] = z.astype(BF16)

    @pl.when(j == 5)
    def _():
        z = proj()
        v_ref[...] = z
        vb_ref[...] = z.astype(BF16)


def _mixin_prompt(x, g, w_main, w_f, b_f, cw, gc, seq, n_heads, scale):
    t, d = x.shape
    c = w_main.shape[1] // 6
    tm = TOKEN_TILE
    row = lambda i, j: (i, 0)
    fixed = lambda i, j: (0, 0)
    kern = functools.partial(_mixin_prompt_kernel, tm=tm, tiles_per_seq=seq // tm,
                             n_heads=n_heads, scale=scale)
    return pl.pallas_call(
        kern,
        grid=(t // tm, 6),
        in_specs=[
            pl.BlockSpec((tm, d), row),
            pl.BlockSpec((1, d), fixed),
            pl.BlockSpec((d, c), lambda i, j: (0, j)),
            pl.BlockSpec((d, LANES), fixed),
            pl.BlockSpec((1, LANES), fixed),
            pl.BlockSpec((CONV_W, c), fixed),
            pl.BlockSpec((1, c), fixed),
        ],
        out_specs=[
            pl.BlockSpec((tm, c), row),
            pl.BlockSpec((tm, c), row),
            pl.BlockSpec((tm, c), row),
            pl.BlockSpec((tm, c), row),
            pl.BlockSpec((tm, c), row),
            pl.BlockSpec((tm, c), row),
            pl.BlockSpec((tm, n_heads), row),
            pl.BlockSpec((tm, LANES), row),
            pl.BlockSpec((1, CONV_W - 1, c), lambda i, j: (i // (seq // tm), 0, 0)),
        ],
        out_shape=[
            jax.ShapeDtypeStruct((t, c), BF16),
            jax.ShapeDtypeStruct((t, c), BF16),
            jax.ShapeDtypeStruct((t, c), F32),
            jax.ShapeDtypeStruct((t, c), F32),
            jax.ShapeDtypeStruct((t, c), BF16),
            jax.ShapeDtypeStruct((t, c), BF16),
            jax.ShapeDtypeStruct((t, n_heads), F32),
            jax.ShapeDtypeStruct((t, LANES), F32),
            jax.ShapeDtypeStruct((t // seq, CONV_W - 1, c), F32),
        ],
        scratch_shapes=[
            pltpu.VMEM((tm, d), BF16),
            pltpu.VMEM((tm, c), F32),
            pltpu.VMEM((tm, c), F32),
            pltpu.VMEM((tm + SUBLANES, c), F32),
        ],
        compiler_params=_params("arbitrary", "arbitrary"),
        name="mixer_in_prompt",
    )(x, g, w_main, w_f, b_f, cw, gc)


def _mixin_sample_kernel(x_ref, g_ref, w_ref, wf_ref, bf_ref, cw_ref, gc_ref, s1_ref, s2_ref,
                         zc_ref, q_ref, k_ref, v_ref, u_ref, lf_ref,
                         hn_ref, cb_ref, cc_ref, uext_ref, *, tm, dec_seq, n_heads, scale):
    j = pl.program_id(1)
    proj = _mixin_common(j, x_ref, g_ref, w_ref, wf_ref, bf_ref, lf_ref, None, hn_ref, cb_ref,
                         n_heads)

    @pl.when(j == 1)
    def _():
        cc_ref[...] = proj()

    @pl.when(j == 2)
    def _():
        uext_ref[pl.ds(0, SUBLANES), :] = jnp.zeros((SUBLANES, cc_ref.shape[1]), F32)
        u = cc_ref[...] * proj()
        um1, um2 = _conv_taps(u, uext_ref, tm)
        step = lax.broadcasted_iota(jnp.int32, u.shape, 0) % dec_seq
        um1 = jnp.where(step >= 1, um1, s1_ref[...])
        um2 = jnp.where(step >= 2, um2, s2_ref[...])
        y = cb_ref[...] * (cw_ref[0:1, :] * um2 + cw_ref[1:2, :] * um1 + cw_ref[2:3, :] * u)
        zc_ref[...] = _rms(y, gc_ref[...]).astype(BF16)
        u_ref[...] = u

    @pl.when(j == 3)
    def _():
        q_ref[...] = proj() * scale

    @pl.when(j == 4)
    def _():
        k_ref[...] = proj()

    @pl.when(j == 5)
    def _():
        v_ref[...] = proj()


def _mixin_sample(x, g, w_main, w_f, b_f, cw, gc, s1, s2, dec_seq, n_heads, scale):
    t, d = x.shape
    c = w_main.shape[1] // 6
    tm = t
    row = lambda i, j: (i, 0)
    fixed = lambda i, j: (0, 0)
    kern = functools.partial(_mixin_sample_kernel, tm=tm, dec_seq=dec_seq, n_heads=n_heads,
                             scale=scale)
    return pl.pallas_call(
        kern,
        grid=(1, 6),
        in_specs=[
            pl.BlockSpec((tm, d), row),
            pl.BlockSpec((1, d), fixed),
            pl.BlockSpec((d, c), lambda i, j: (0, j)),
            pl.BlockSpec((d, LANES), fixed),
            pl.BlockSpec((1, LANES), fixed),
            pl.BlockSpec((CONV_W, c), fixed),
            pl.BlockSpec((1, c), fixed),
            pl.BlockSpec((tm, c), row),
            pl.BlockSpec((tm, c), row),
        ],
        out_specs=[
            pl.BlockSpec((tm, c), row),
            pl.BlockSpec((tm, c), row),
            pl.BlockSpec((tm, c), row),
            pl.BlockSpec((tm, c), row),
            pl.BlockSpec((tm, c), row),
            pl.BlockSpec((tm, n_heads), row),
        ],
        out_shape=[
            jax.ShapeDtypeStruct((t, c), BF16),
            jax.ShapeDtypeStruct((t, c), F32),
            jax.ShapeDtypeStruct((t, c), F32),
            jax.ShapeDtypeStruct((t, c), F32),
            jax.ShapeDtypeStruct((t, c), F32),
            jax.ShapeDtypeStruct((t, n_heads), F32),
        ],
        scratch_shapes=[
            pltpu.VMEM((tm, d), BF16),
            pltpu.VMEM((tm, c), F32),
            pltpu.VMEM((tm, c), F32),
            pltpu.VMEM((tm + SUBLANES, c), F32),
        ],
        compiler_params=_params("arbitrary", "arbitrary"),
        name="mixer_in_sample",
    )(x, g, w_main, w_f, b_f, cw, gc, s1, s2)


def _split3(x):
    hi = x.astype(BF16)
    r1 = x - hi.astype(F32)
    mid = r1.astype(BF16)
    lo = (r1 - mid.astype(F32)).astype(BF16)
    return hi, mid, lo


def _attn_prompt_kernel(q_ref, k_ref, v_ref, lfrep_ref, ga_ref, za_ref, aug_ref, o_ref,
                        *, blk, n_heads):
    i = pl.program_id(1)
    seq = k_ref.shape[0]
    lane = lax.broadcasted_iota(jnp.int32, (blk, LANES), 1)
    row = lax.broadcasted_iota(jnp.int32, (blk, blk), 0)
    col = lax.broadcasted_iota(jnp.int32, (blk, blk), 1)
    causal = col <= row

    @pl.when(i == 0)
    def _():
        tri = jnp.where(causal, 1.0, 0.0).astype(BF16)
        carry = jnp.zeros((1, LANES), F32)
        for jb in range(seq // blk):
            rows = slice(jb * blk, (jb + 1) * blk)
            hi, mid, lo = _split3(lfrep_ref[rows, :])
            f = _dot(tri, hi) + _dot(tri, mid) + _dot(tri, lo) + carry
            carry = f[blk - 1:blk, :]
            hi, mid, lo = _split3(f * (-LOG2E))
            zero = jnp.zeros_like(hi)
            aug_ref[rows, :] = jnp.where(
                lane < n_heads, hi,
                jnp.where(lane < 2 * n_heads, mid, jnp.where(lane < 3 * n_heads, lo, zero)))

    def step(jb, carries, heads, q_aug, masked):
        start = pl.multiple_of(jb * blk, blk)
        aj = aug_ref[pl.ds(start, blk), :]
        out = []
        for h, qa, (m, l, acc) in zip(heads, q_aug, carries):
            hs = slice(h * HEAD_DIM, (h + 1) * HEAD_DIM)
            kj = jnp.concatenate([k_ref[pl.ds(start, blk), hs], aj], axis=1)
            s = _dot_nt(qa, kj)
            if masked:
                s = jnp.where(causal, s, MASK_VALUE)
            m_new = jnp.maximum(m, jnp.max(s, axis=-1, keepdims=True))
            alpha = jnp.exp2(m - m_new)
            p = jnp.exp2(s - m_new)
            l = alpha * l + jnp.sum(p, axis=-1, keepdims=True)
            acc = alpha * acc + _dot(p.astype(BF16), v_ref[pl.ds(start, blk), hs])
            out.append((m_new, l, acc))
        return tuple(out)

    for h0 in range(0, n_heads, HEAD_GROUP):
        heads = tuple(range(h0, h0 + HEAD_GROUP))
        q_aug = []
        for h in heads:
            ones = jnp.where((lane % n_heads == h) & (lane < 3 * n_heads), 1.0, 0.0).astype(BF16)
            q_aug.append(jnp.concatenate([q_ref[:, h * HEAD_DIM:(h + 1) * HEAD_DIM], ones], axis=1))
        init = tuple((jnp.full((blk, 1), MASK_VALUE, F32), jnp.zeros((blk, 1), F32),
                      jnp.zeros((blk, HEAD_DIM), F32)) for _ in heads)
        carries = lax.fori_loop(0, i, lambda jb, c: step(jb, c, heads, q_aug, False), init)
        carries = step(i, carries, heads, q_aug, True)
        for h, (m, l, acc) in zip(heads, carries):
            o_ref[:, h * HEAD_DIM:(h + 1) * HEAD_DIM] = acc / l

    za_ref[...] = _rms(o_ref[...], ga_ref[...]).astype(BF16)


def _attn_prompt(q, kb, vb, lfrep, ga, seq, n_heads):
    t, c = q.shape
    b = t // seq
    blk = ATTN_BLOCK
    nq = seq // blk
    kern = functools.partial(_attn_prompt_kernel, blk=blk, n_heads=n_heads)
    return pl.pallas_call(
        kern,
        grid=(b, nq),
        in_specs=[
            pl.BlockSpec((blk, c), lambda bi, i: (bi * nq + i, 0)),
            pl.BlockSpec((seq, c), lambda bi, i: (bi, 0)),
            pl.BlockSpec((seq, c), lambda bi, i: (bi, 0)),
            pl.BlockSpec((seq, LANES), lambda bi, i: (bi, 0)),
            pl.BlockSpec((1, c), lambda bi, i: (0, 0)),
        ],
        out_specs=pl.BlockSpec((blk, c), lambda bi, i: (bi * nq + i, 0)),
        out_shape=jax.ShapeDtypeStruct((t, c), BF16),
        scratch_shapes=[pltpu.VMEM((seq, LANES), BF16), pltpu.VMEM((blk, c), F32)],
        compiler_params=_params("arbitrary", "arbitrary"),
        name="fox_prompt",
    )(q, kb, vb, lfrep, ga)


def _split3_dot(x, w):
    hi, mid, lo = _split3(x)
    return _dot(hi, w) + _dot(mid, w) + _dot(lo, w)


def _select_sum_kernel(x_ref, w_ref, o_ref):
    o_ref[...] = _split3_dot(x_ref[...], w_ref[...])


def _select_sum(x, w, tm):
    rows, k = x.shape
    n = w.shape[1]
    return pl.pallas_call(
        _select_sum_kernel,
        grid=(rows // tm,),
        in_specs=[pl.BlockSpec((tm, k), lambda i: (i, 0)), pl.BlockSpec((k, n), lambda i: (0, 0))],
        out_specs=pl.BlockSpec((tm, n), lambda i: (i, 0)),
        out_shape=jax.ShapeDtypeStruct((rows, n), F32),
        compiler_params=_params("parallel"),
        name="head_major_prefix_sum",
    )(x, w)


def _prefix_matrix(n_steps, n_heads, width):
    r = jnp.arange(n_steps * n_heads)
    cidx = jnp.arange(n_heads * width)
    t, h = r // n_heads, r % n_heads
    h2, t2 = cidx // width, cidx % width
    w = (h[:, None] == h2[None, :]) & (t[:, None] <= t2[None, :]) & (t2[None, :] < n_steps)
    return w.astype(BF16)


def _attn_sample_kernel(pt_ref, q_ref, kn_ref, vn_ref, fn_ref, ga_ref, *rest,
                        pages, n_heads, dec_seq):
    k_refs = rest[:pages]
    v_refs = rest[pages:2 * pages]
    pf_refs = rest[2 * pages:3 * pages]
    za_ref = rest[3 * pages]
    qbd_ref, kb_ref, vb_ref, m_ref, l_ref, acc_ref, run_ref = rest[3 * pages + 1:]
    g = pl.program_id(1)
    rows = dec_seq * n_heads
    c = n_heads * HEAD_DIM
    head_of_row = lax.broadcasted_iota(jnp.int32, (n_heads, c), 0)
    head_of_col = lax.broadcasted_iota(jnp.int32, (n_heads, c), 1) // HEAD_DIM
    diag = head_of_row == head_of_col

    @pl.when(g == 0)
    def _():
        for t in range(dec_seq):
            qt = jnp.broadcast_to(q_ref[0, t:t + 1, :], (n_heads, c))
            qbd_ref[t * n_heads:(t + 1) * n_heads, :] = jnp.where(diag, qt, 0.0).astype(BF16)
        m_ref[...] = jnp.full(m_ref.shape, MASK_VALUE, F32)
        l_ref[...] = jnp.zeros(l_ref.shape, F32)
        acc_ref[...] = jnp.zeros(acc_ref.shape, F32)
        run_ref[...] = jnp.zeros(run_ref.shape, F32)

    def online_update(s, v_bf):
        m_old = m_ref[...]
        m_new = jnp.maximum(m_old, jnp.max(s, axis=-1, keepdims=True))
        alpha = jnp.exp(m_old - m_new)
        p = jnp.exp(s - m_new)
        l_ref[...] = alpha * l_ref[...] + jnp.sum(p, axis=-1, keepdims=True)
        acc_ref[...] = alpha * acc_ref[...] + _dot(p.astype(BF16), v_bf)
        m_ref[...] = m_new

    run = run_ref[...]
    bias_parts = []
    for pg in range(pages):
        toks = slice(pg * PAGE_SIZE, (pg + 1) * PAGE_SIZE)
        for h in range(n_heads):
            hs = slice(h * HEAD_DIM, (h + 1) * HEAD_DIM)
            head_rows = pl.ds(h, PAGE_SIZE, stride=n_heads)
            kb_ref[toks, hs] = k_refs[pg][0, 0, head_rows, :].astype(BF16)
            vb_ref[toks, hs] = v_refs[pg][0, 0, head_rows, :].astype(BF16)
        pf = pf_refs[pg][0]
        bias_parts.append(run + pf)
        run = run + jnp.broadcast_to(pf[:, PAGE_SIZE - 1:PAGE_SIZE], pf.shape)
    run_ref[...] = run
    bias = jnp.concatenate(bias_parts, axis=1)
    bias = jnp.concatenate([bias] * dec_seq, axis=0)
    s = _dot_nt(qbd_ref[...], kb_ref[...]) - bias
    online_update(s, vb_ref[...])

    @pl.when(g == pl.num_programs(1) - 1)
    def _():
        pad = jnp.zeros((PAGE_SIZE - SUBLANES, c), F32)
        kn = jnp.concatenate([kn_ref[0], pad], axis=0).astype(BF16)
        vn = jnp.concatenate([vn_ref[0], pad], axis=0).astype(BF16)
        bias_n = jnp.concatenate([run + fn_ref[0]] * dec_seq, axis=0)
        s_n = _dot_nt(qbd_ref[...], kn) - bias_n
        q_step = lax.broadcasted_iota(jnp.int32, (rows, PAGE_SIZE), 0) // n_heads
        k_step = lax.broadcasted_iota(jnp.int32, (rows, PAGE_SIZE), 1)
        s_n = jnp.where(k_step <= q_step, s_n, MASK_VALUE)
        online_update(s_n, vn)
        o = acc_ref[...] / l_ref[...]
        for t in range(dec_seq):
            ot = jnp.where(diag, o[t * n_heads:(t + 1) * n_heads, :], 0.0)
            ot = jnp.sum(ot, axis=0, keepdims=True)
            za_ref[0, t:t + 1, :] = _rms(ot, ga_ref[...])


def _attn_sample(page_table, q, k_new, v_new, f_new, ga, cache_k, cache_v, pfx, layer, n_heads,
                 dec_seq):
    nb, n_pages = page_table.shape
    c = n_heads * HEAD_DIM
    pages = PAGES_PER_STEP
    rows = dec_seq * n_heads
    per_b = lambda b, g, pt: (b, 0, 0)
    page = lambda r: (lambda b, g, pt: (pt[b, g * pages + r], 0, 0))
    kv_page = lambda r: (lambda b, g, pt: (layer, pt[b, g * pages + r], 0, 0))
    kv_block = (1, 1, PAGE_SIZE * n_heads, HEAD_DIM)
    kern = functools.partial(_attn_sample_kernel, pages=pages, n_heads=n_heads, dec_seq=dec_seq)
    grid_spec = pltpu.PrefetchScalarGridSpec(
        num_scalar_prefetch=1,
        grid=(nb, n_pages // pages),
        in_specs=(
            [pl.BlockSpec((1, dec_seq, c), per_b),
             pl.BlockSpec((1, SUBLANES, c), per_b),
             pl.BlockSpec((1, SUBLANES, c), per_b),
             pl.BlockSpec((1, n_heads, LANES), per_b),
             pl.BlockSpec((1, c), lambda b, g, pt: (0, 0))]
            + [pl.BlockSpec(kv_block, kv_page(r)) for r in range(pages)]
            + [pl.BlockSpec(kv_block, kv_page(r)) for r in range(pages)]
            + [pl.BlockSpec((1, n_heads, LANES), page(r)) for r in range(pages)]
        ),
        out_specs=pl.BlockSpec((1, dec_seq, c), per_b),
        scratch_shapes=[
            pltpu.VMEM((rows, c), BF16),
            pltpu.VMEM((pages * PAGE_SIZE, c), BF16),
            pltpu.VMEM((pages * PAGE_SIZE, c), BF16),
            pltpu.VMEM((rows, 1), F32),
            pltpu.VMEM((rows, 1), F32),
            pltpu.VMEM((rows, c), F32),
            pltpu.VMEM((n_heads, LANES), F32),
        ],
    )
    return pl.pallas_call(
        kern,
        grid_spec=grid_spec,
        out_shape=jax.ShapeDtypeStruct((nb, dec_seq, c), F32),
        compiler_params=_params("arbitrary", "arbitrary"),
        name="fox_sample",
    )(page_table, q, k_new, v_new, f_new, ga,
      *([cache_k] * pages), *([cache_v] * pages), *([pfx] * pages))


def _mixout_kernel(h_ref, zc_ref, za_ref, wc_ref, wa_ref, o_ref):
    o_ref[...] = (h_ref[...] + _dot(zc_ref[...].astype(BF16), wc_ref[...])
                  + _dot(za_ref[...].astype(BF16), wa_ref[...]))


def _mixout(h, zc, za, wc, wa, tm):
    t, d = h.shape
    c = zc.shape[1]
    row = lambda i: (i, 0)
    fixed = lambda i: (0, 0)
    return pl.pallas_call(
        _mixout_kernel,
        grid=(t // tm,),
        in_specs=[pl.BlockSpec((tm, d), row), pl.BlockSpec((tm, c), row), pl.BlockSpec((tm, c), row),
                  pl.BlockSpec((c, d), fixed), pl.BlockSpec((za.shape[1], d), fixed)],
        out_specs=pl.BlockSpec((tm, d), row),
        out_shape=jax.ShapeDtypeStruct((t, d), F32),
        compiler_params=_params("parallel"),
        name="mixer_out",
    )(h, zc, za, wc, wa)


def _ple_kernel(h_ref, p_ref, g_ref, wg_ref, wp_ref, gf_ref, o_ref):
    h = h_ref[...]
    hn = _rms(h, g_ref[...]).astype(BF16)
    gate = jax.nn.sigmoid(_dot(hn, wg_ref[...]))
    proj = _dot(p_ref[...].astype(BF16), wp_ref[...])
    o_ref[...] = _rms(h + gate * proj, gf_ref[...])


def _ple_final(h, p, g, wg, wp, gf, tm):
    t, d = h.shape
    pd = p.shape[1]
    row = lambda i: (i, 0)
    fixed = lambda i: (0, 0)
    return pl.pallas_call(
        _ple_kernel,
        grid=(t // tm,),
        in_specs=[pl.BlockSpec((tm, d), row), pl.BlockSpec((tm, pd), row), pl.BlockSpec((1, d), fixed),
                  pl.BlockSpec((d, d), fixed), pl.BlockSpec((pd, d), fixed), pl.BlockSpec((1, d), fixed)],
        out_specs=pl.BlockSpec((tm, d), row),
        out_shape=jax.ShapeDtypeStruct((t, d), F32),
        compiler_params=_params("parallel"),
        name="ple_final_norm",
    )(h, p, g, wg, wp, gf)


def _ple_kernel_nofinal(h_ref, p_ref, g_ref, wg_ref, wp_ref, o_ref):
    h = h_ref[...]
    hn = _rms(h, g_ref[...]).astype(BF16)
    gate = jax.nn.sigmoid(_dot(hn, wg_ref[...]))
    o_ref[...] = h + gate * _dot(p_ref[...].astype(BF16), wp_ref[...])


def _ple(h, p, g, wg, wp, tm):
    t, d = h.shape
    pd = p.shape[1]
    row = lambda i: (i, 0)
    fixed = lambda i: (0, 0)
    return pl.pallas_call(
        _ple_kernel_nofinal,
        grid=(t // tm,),
        in_specs=[pl.BlockSpec((tm, d), row), pl.BlockSpec((tm, pd), row), pl.BlockSpec((1, d), fixed),
                  pl.BlockSpec((d, d), fixed), pl.BlockSpec((pd, d), fixed)],
        out_specs=pl.BlockSpec((tm, d), row),
        out_shape=jax.ShapeDtypeStruct((t, d), F32),
        compiler_params=_params("parallel"),
        name="ple",
    )(h, p, g, wg, wp)


def _final_norm_kernel(h_ref, g_ref, o_ref):
    o_ref[...] = _rms(h_ref[...], g_ref[...])


def _final_norm(h, g, tm):
    t, d = h.shape
    return pl.pallas_call(
        _final_norm_kernel,
        grid=(t // tm,),
        in_specs=[pl.BlockSpec((tm, d), lambda i: (i, 0)), pl.BlockSpec((1, d), lambda i: (0, 0))],
        out_specs=pl.BlockSpec((tm, d), lambda i: (i, 0)),
        out_shape=jax.ShapeDtypeStruct((t, d), F32),
        compiler_params=_params("parallel"),
        name="final_norm",
    )(h, g)


def kernel(x_prompt, x_sample, p_prompt, p_sample, cache_k, cache_v, cache_logf, state_conv, page_table, norm_ffn1, w_ffn1_gate, w_ffn1_up, w_ffn1_down, norm_mix, w_in, b_f, conv_w, norm_conv_out, norm_attn_out, w_out, norm_ffn2, w_ffn2_gate, w_ffn2_up, w_ffn2_down, norm_ple, w_ple_gate, w_ple_proj, norm_final):
    batch, seq, d = x_prompt.shape
    dec_batch, dec_seq, _ = x_sample.shape
    depth = w_in.shape[0]
    n_heads = b_f.shape[1]
    conv_dim = conv_w.shape[2]
    attn_dim = n_heads * HEAD_DIM
    n_pool = cache_k.shape[1]
    scale = HEAD_DIM ** -0.5
    n_main = 3 * conv_dim + 3 * attn_dim
    assert conv_dim == attn_dim and dec_seq >= CONV_W - 1 and dec_seq <= SUBLANES
    tp, ts = batch * seq, dec_batch * dec_seq

    hp = x_prompt.reshape(tp, d)
    hs = x_sample.reshape(ts, d)
    row2 = lambda a: a.reshape(1, -1)
    page_prefix_w = _prefix_matrix(PAGE_SIZE, n_heads, PAGE_SIZE)
    new_prefix_w = _prefix_matrix(dec_seq, n_heads, LANES)

    outs = [[] for _ in range(8)]
    for l in range(depth):
        w_main = w_in[l, :, :n_main].astype(BF16)
        w_f = jnp.tile(w_in[l, :, n_main:], (1, LANES // n_heads)).astype(BF16)
        w_oc = w_out[l, :conv_dim].astype(BF16)
        w_oa = w_out[l, conv_dim:].astype(BF16)
        w_pg = w_ple_gate[l].astype(BF16)
        w_pp = w_ple_proj[l].astype(BF16)
        g1, gm, g2, gp = row2(norm_ffn1[l]), row2(norm_mix[l]), row2(norm_ffn2[l]), row2(norm_ple[l])
        gc, ga = row2(norm_conv_out[l]), row2(norm_attn_out[l])
        bf = row2(jnp.tile(b_f[l], LANES // n_heads))
        cw = conv_w[l]
        last = l == depth - 1
        gfin = row2(norm_final)

        hp, hs = _ffn(hp, hs, g1, w_ffn1_gate, w_ffn1_up, w_ffn1_down, l)

        zc, q, k, v, kb, vb, lf, lfrep, cs = _mixin_prompt(hp, gm, w_main, w_f, bf, cw, gc, seq,
                                                           n_heads, scale * LOG2E)
        za = _attn_prompt(q, kb, vb, lfrep, ga, seq, n_heads)
        hp = _mixout(hp, zc, za, w_oc, w_oa, TOKEN_TILE)
        outs[0].append(k.reshape(batch, seq, n_heads, HEAD_DIM))
        outs[1].append(v.reshape(batch, seq, n_heads, HEAD_DIM))
        outs[2].append(lf.reshape(batch, seq, n_heads))
        outs[3].append(cs)

        st = state_conv[l]
        zero = jnp.zeros((dec_batch, dec_seq - 1, conv_dim), F32)
        s1 = jnp.concatenate([st[:, 1:2], zero], axis=1).reshape(ts, conv_dim)
        s2 = jnp.concatenate([st, zero[:, 1:]], axis=1).reshape(ts, conv_dim)
        zc, q, k, v, u, lf = _mixin_sample(hs, gm, w_main, w_f, bf, cw, gc, s1, s2,
                                           dec_seq, n_heads, scale)
        pfx = _select_sum(cache_logf[l].reshape(n_pool, PAGE_SIZE * n_heads), page_prefix_w, 512)
        pfx = pfx.reshape(n_pool, n_heads, PAGE_SIZE)
        f_new = _select_sum(lf.reshape(dec_batch, dec_seq * n_heads), new_prefix_w, dec_batch)
        f_new = f_new.reshape(dec_batch, n_heads, LANES)
        pad8 = lambda a: jnp.pad(a.reshape(dec_batch, dec_seq, attn_dim),
                                 ((0, 0), (0, SUBLANES - dec_seq), (0, 0)))
        za = _attn_sample(page_table, q.reshape(dec_batch, dec_seq, attn_dim), pad8(k), pad8(v),
                          f_new, ga,
                          cache_k.reshape(depth, n_pool, PAGE_SIZE * n_heads, HEAD_DIM),
                          cache_v.reshape(depth, n_pool, PAGE_SIZE * n_heads, HEAD_DIM),
                          pfx, l, n_heads, dec_seq)
        hs = _mixout(hs, zc, za.reshape(ts, attn_dim), w_oc, w_oa, ts)

        hp, hs = _ffn(hp, hs, g2, w_ffn2_gate, w_ffn2_up, w_ffn2_down, l)
        pp = p_prompt[l].reshape(tp, -1)
        hp = (_ple_final(hp, pp, gp, w_pg, w_pp, gfin, TOKEN_TILE) if last
              else _ple(hp, pp, gp, w_pg, w_pp, TOKEN_TILE))
        ps = p_sample[l].reshape(ts, -1)
        hs = (_ple_final(hs, ps, gp, w_pg, w_pp, gfin, ts) if last
              else _ple(hs, ps, gp, w_pg, w_pp, ts))
        outs[4].append(k.reshape(dec_batch, dec_seq, n_heads, HEAD_DIM))
        outs[5].append(v.reshape(dec_batch, dec_seq, n_heads, HEAD_DIM))
        outs[6].append(lf.reshape(dec_batch, dec_seq, n_heads))
        outs[7].append(u.reshape(dec_batch, dec_seq, conv_dim)[:, dec_seq - (CONV_W - 1):])

    return (hp.reshape(batch, seq, d), hs.reshape(dec_batch, dec_seq, d),
            *(jnp.stack(o) for o in outs))
```

```python
import functools

import jax
import jax.numpy as jnp
from jax import lax
from jax.experimental import pallas as pl
from jax.experimental.pallas import tpu as pltpu

F32 = jnp.float32
BF16 = jnp.bfloat16

EPS = 1e-6
HEAD_DIM = 128
PAGE_SIZE = 128
CONV_W = 3
LANES = 128
SUBLANES = 8
MASK_VALUE = -1e30
VMEM_LIMIT = 56 * 1024 * 1024

TOKEN_TILE = 512
FFN_TOKEN_TILE = 1024
FFN_FF_TILE = 256
ATTN_BLOCK = 256
HEAD_GROUP = 8
LOG2E = 1.4426950408889634
PAGES_PER_STEP = 16


def _params(*sem):
    return pltpu.CompilerParams(dimension_semantics=sem, vmem_limit_bytes=VMEM_LIMIT)


def _rms(x, g):
    ms = jnp.mean(x * x, axis=-1, keepdims=True)
    return x * lax.rsqrt(ms + EPS) * g


def _dot(a, b):
    return jnp.dot(a, b, preferred_element_type=F32)


def _dot_nt(a, b):
    return lax.dot_general(a, b, (((1,), (1,)), ((), ())), preferred_element_type=F32)


def _log_sigmoid(x):
    return jnp.minimum(x, 0.0) - jnp.log1p(jnp.exp(-jnp.abs(x)))


def _ffn_kernel(x_hbm, xs_ref, g_ref, wg_ref, wu_ref, wd_ref, o_ref, os_ref,
                xbuf_ref, hn_ref, hns_ref, sem, *, tm):
    i = pl.program_id(0)
    j = pl.program_id(1)
    last_tile = pl.num_programs(0) - 1

    def x_copy(tile):
        return pltpu.make_async_copy(x_hbm.at[pl.ds(tile * tm, tm), :], xbuf_ref, sem)

    @pl.when((i == 0) & (j == 0))
    def _():
        x_copy(0).start()

    @pl.when(j == 0)
    def _():
        x_copy(i).wait()
        x = xbuf_ref[...]
        hn_ref[...] = _rms(x, g_ref[...]).astype(BF16)
        o_ref[...] = x

    @pl.when((j == 1) & (i < last_tile))
    def _():
        x_copy(i + 1).start()

    wg = wg_ref[...].astype(BF16)
    wu = wu_ref[...].astype(BF16)
    wd = wd_ref[...].astype(BF16)

    def half_step(hn):
        gate = _dot(hn, wg)
        up = _dot(hn, wu)
        act = (gate * jax.nn.sigmoid(gate) * (0.5 * up)).astype(BF16)
        return _dot(act, wd)

    o_ref[...] += half_step(hn_ref[...])

    @pl.when(i == last_tile)
    def _():
        @pl.when(j == 0)
        def _():
            xs = xs_ref[...]
            hns_ref[...] = _rms(xs, g_ref[...]).astype(BF16)
            os_ref[...] = xs

        os_ref[...] += half_step(hns_ref[...])


def _ffn(x, xs, g, wg, wu, wd, layer):
    t, d = x.shape
    ts = xs.shape[0]
    f = wg.shape[2]
    tm, tf = FFN_TOKEN_TILE, FFN_FF_TILE
    assert f % tf == 0 and f // tf >= 2 and t % tm == 0
    return pl.pallas_call(
        functools.partial(_ffn_kernel, tm=tm),
        grid=(t // tm, f // tf),
        in_specs=[
            pl.BlockSpec(memory_space=pl.ANY),
            pl.BlockSpec((ts, d), lambda i, j: (0, 0)),
            pl.BlockSpec((1, d), lambda i, j: (0, 0)),
            pl.BlockSpec((None, d, tf), lambda i, j: (layer, 0, j)),
            pl.BlockSpec((None, d, tf), lambda i, j: (layer, 0, j)),
            pl.BlockSpec((None, tf, d), lambda i, j: (layer, j, 0)),
        ],
        out_specs=[pl.BlockSpec((tm, d), lambda i, j: (i, 0)),
                   pl.BlockSpec((ts, d), lambda i, j: (0, 0))],
        out_shape=[jax.ShapeDtypeStruct((t, d), F32), jax.ShapeDtypeStruct((ts, d), F32)],
        scratch_shapes=[pltpu.VMEM((tm, d), F32), pltpu.VMEM((tm, d), BF16),
                        pltpu.VMEM((ts, d), BF16), pltpu.SemaphoreType.DMA(())],
        compiler_params=_params("arbitrary", "arbitrary"),
        name="ffn_half",
    )(x, xs, g, wg, wu, wd)


def _mixin_common(j, x_ref, g_ref, w_ref, wf_ref, bf_ref, lf_ref, lfrep_ref, hn_ref, cb_ref,
                  n_heads):
    @pl.when(j == 0)
    def _():
        hn = _rms(x_ref[...], g_ref[...]).astype(BF16)
        hn_ref[...] = hn
        cb_ref[...] = _dot(hn, w_ref[...])
        lf = _log_sigmoid(_dot(hn, wf_ref[...]) + bf_ref[...])
        lf_ref[...] = lf[:, :n_heads]
        if lfrep_ref is not None:
            lfrep_ref[...] = lf

    return lambda: _dot(hn_ref[...], w_ref[...])


def _conv_taps(u, uext_ref, tm):
    uext_ref[pl.ds(SUBLANES, tm), :] = u
    um1 = uext_ref[pl.ds(SUBLANES - 1, tm), :]
    um2 = uext_ref[pl.ds(SUBLANES - 2, tm), :]
    return um1, um2


def _mixin_prompt_kernel(x_ref, g_ref, w_ref, wf_ref, bf_ref, cw_ref, gc_ref,
                         zc_ref, q_ref, k_ref, v_ref, kb_ref, vb_ref, lf_ref, lfrep_ref, cs_ref,
                         hn_ref, cb_ref, cc_ref, uext_ref, *, tm, tiles_per_seq, n_heads, scale):
    i = pl.program_id(0)
    j = pl.program_id(1)
    proj = _mixin_common(j, x_ref, g_ref, w_ref, wf_ref, bf_ref, lf_ref, lfrep_ref, hn_ref, cb_ref,
                         n_heads)

    @pl.when(j == 1)
    def _():
        cc_ref[...] = proj()

    @pl.when(j == 2)
    def _():
        @pl.when(i % tiles_per_seq == 0)
        def _():
            uext_ref[pl.ds(0, SUBLANES), :] = jnp.zeros((SUBLANES, cc_ref.shape[1]), F32)

        u = cc_ref[...] * proj()
        um1, um2 = _conv_taps(u, uext_ref, tm)
        y = cb_ref[...] * (cw_ref[0:1, :] * um2 + cw_ref[1:2, :] * um1 + cw_ref[2:3, :] * u)
        zc_ref[...] = _rms(y, gc_ref[...]).astype(BF16)
        cs_ref[0] = u[tm - (CONV_W - 1):, :]
        uext_ref[pl.ds(0, SUBLANES), :] = u[tm - SUBLANES:, :]

    @pl.when(j == 3)
    def _():
        q_ref[...] = (proj() * scale).astype(BF16)

    @pl.when(j == 4)
    def _():
        z = proj()
        k_ref[...] = z
        kb_ref[...] = z.astype(BF16)

    @pl.when(j == 5)
    def _():
        z = proj()
        v_ref[...] = z
        vb_ref[...] = z.astype(BF16)


def _mixin_prompt(x, g, w_main, w_f, b_f, cw, gc, seq, n_heads, scale):
    t, d = x.shape
    c = cw.shape[1]
    tm = TOKEN_TILE
    row = lambda i, j: (i, 0)
    fixed = lambda i, j: (0, 0)
    kern = functools.partial(_mixin_prompt_kernel, tm=tm, tiles_per_seq=seq // tm,
                             n_heads=n_heads, scale=scale)
    return pl.pallas_call(
        kern,
        grid=(t // tm, 6),
        in_specs=[
            pl.BlockSpec((tm, d), row),
            pl.BlockSpec((1, d), fixed),
            pl.BlockSpec((d, c), lambda i, j: (0, j)),
            pl.BlockSpec((d, LANES), fixed),
            pl.BlockSpec((1, LANES), fixed),
            pl.BlockSpec((CONV_W, c), fixed),
            pl.BlockSpec((1, c), fixed),
        ],
        out_specs=[
            pl.BlockSpec((tm, c), row),
            pl.BlockSpec((tm, c), row),
            pl.BlockSpec((tm, c), row),
            pl.BlockSpec((tm, c), row),
            pl.BlockSpec((tm, c), row),
            pl.BlockSpec((tm, c), row),
            pl.BlockSpec((tm, n_heads), row),
            pl.BlockSpec((tm, LANES), row),
            pl.BlockSpec((1, CONV_W - 1, c), lambda i, j: (i // (seq // tm), 0, 0)),
        ],
        out_shape=[
            jax.ShapeDtypeStruct((t, c), BF16),
            jax.ShapeDtypeStruct((t, c), BF16),
            jax.ShapeDtypeStruct((t, c), F32),
            jax.ShapeDtypeStruct((t, c), F32),
            jax.ShapeDtypeStruct((t, c), BF16),
            jax.ShapeDtypeStruct((t, c), BF16),
            jax.ShapeDtypeStruct((t, n_heads), F32),
            jax.ShapeDtypeStruct((t, LANES), F32),
            jax.ShapeDtypeStruct((t // seq, CONV_W - 1, c), F32),
        ],
        scratch_shapes=[
            pltpu.VMEM((tm, d), BF16),
            pltpu.VMEM((tm, c), F32),
            pltpu.VMEM((tm, c), F32),
            pltpu.VMEM((tm + SUBLANES, c), F32),
        ],
        compiler_params=_params("arbitrary", "arbitrary"),
        name="mixer_in_prompt",
    )(x, g, w_main, w_f, b_f, cw, gc)


def _mixin_sample_kernel(x_ref, g_ref, w_ref, wf_ref, bf_ref, cw_ref, gc_ref, s1_ref, s2_ref,
                         zc_ref, q_ref, k_ref, v_ref, u_ref, lf_ref,
                         hn_ref, cb_ref, cc_ref, uext_ref, *, tm, dec_seq, n_heads, scale):
    j = pl.program_id(1)
    proj = _mixin_common(j, x_ref, g_ref, w_ref, wf_ref, bf_ref, lf_ref, None, hn_ref, cb_ref,
                         n_heads)

    @pl.when(j == 1)
    def _():
        cc_ref[...] = proj()

    @pl.when(j == 2)
    def _():
        uext_ref[pl.ds(0, SUBLANES), :] = jnp.zeros((SUBLANES, cc_ref.shape[1]), F32)
        u = cc_ref[...] * proj()
        um1, um2 = _conv_taps(u, uext_ref, tm)
        step = lax.broadcasted_iota(jnp.int32, u.shape, 0) % dec_seq
        um1 = jnp.where(step >= 1, um1, s1_ref[...])
        um2 = jnp.where(step >= 2, um2, s2_ref[...])
        y = cb_ref[...] * (cw_ref[0:1, :] * um2 + cw_ref[1:2, :] * um1 + cw_ref[2:3, :] * u)
        zc_ref[...] = _rms(y, gc_ref[...]).astype(BF16)
        u_ref[...] = u

    @pl.when(j == 3)
    def _():
        q_ref[...] = proj() * scale

    @pl.when(j == 4)
    def _():
        k_ref[...] = proj()

    @pl.when(j == 5)
    def _():
        v_ref[...] = proj()


def _mixin_sample(x, g, w_main, w_f, b_f, cw, gc, s1, s2, dec_seq, n_heads, scale):
    t, d = x.shape
    c = cw.shape[1]
    tm = t
    row = lambda i, j: (i, 0)
    fixed = lambda i, j: (0, 0)
    kern = functools.partial(_mixin_sample_kernel, tm=tm, dec_seq=dec_seq, n_heads=n_heads,
                             scale=scale)
    return pl.pallas_call(
        kern,
        grid=(1, 6),
        in_specs=[
            pl.BlockSpec((tm, d), row),
            pl.BlockSpec((1, d), fixed),
            pl.BlockSpec((d, c), lambda i, j: (0, j)),
            pl.BlockSpec((d, LANES), fixed),
            pl.BlockSpec((1, LANES), fixed),
            pl.BlockSpec((CONV_W, c), fixed),
            pl.BlockSpec((1, c), fixed),
            pl.BlockSpec((tm, c), row),
            pl.BlockSpec((tm, c), row),
        ],
        out_specs=[
            pl.BlockSpec((tm, c), row),
            pl.BlockSpec((tm, c), row),
            pl.BlockSpec((tm, c), row),
            pl.BlockSpec((tm, c), row),
            pl.BlockSpec((tm, c), row),
            pl.BlockSpec((tm, n_heads), row),
        ],
        out_shape=[
            jax.ShapeDtypeStruct((t, c), BF16),
            jax.ShapeDtypeStruct((t, c), F32),
            jax.ShapeDtypeStruct((t, c), F32),
            jax.ShapeDtypeStruct((t, c), F32),
            jax.ShapeDtypeStruct((t, c), F32),
            jax.ShapeDtypeStruct((t, n_heads), F32),
        ],
        scratch_shapes=[
            pltpu.VMEM((tm, d), BF16),
            pltpu.VMEM((tm, c), F32),
            pltpu.VMEM((tm, c), F32),
            pltpu.VMEM((tm + SUBLANES, c), F32),
        ],
        compiler_params=_params("arbitrary", "arbitrary"),
        name="mixer_in_sample",
    )(x, g, w_main, w_f, b_f, cw, gc, s1, s2)


def _split3(x):
    hi = x.astype(BF16)
    r1 = x - hi.astype(F32)
    mid = r1.astype(BF16)
    lo = (r1 - mid.astype(F32)).astype(BF16)
    return hi, mid, lo


def _attn_prompt_kernel(q_ref, k_ref, v_ref, lfrep_ref, ga_ref, za_ref, aug_ref, o_ref,
                        *, blk, n_heads):
    i = pl.program_id(1)
    seq = k_ref.shape[0]
    lane = lax.broadcasted_iota(jnp.int32, (blk, LANES), 1)
    row = lax.broadcasted_iota(jnp.int32, (blk, blk), 0)
    col = lax.broadcasted_iota(jnp.int32, (blk, blk), 1)
    causal = col <= row

    @pl.when(i == 0)
    def _():
        tri = jnp.where(causal, 1.0, 0.0).astype(BF16)
        carry = jnp.zeros((1, LANES), F32)
        for jb in range(seq // blk):
            rows = slice(jb * blk, (jb + 1) * blk)
            hi, mid, lo = _split3(lfrep_ref[rows, :])
            f = _dot(tri, hi) + _dot(tri, mid) + _dot(tri, lo) + carry
            carry = f[blk - 1:blk, :]
            hi, mid, lo = _split3(f * (-LOG2E))
            zero = jnp.zeros_like(hi)
            aug_ref[rows, :] = jnp.where(
                lane < n_heads, hi,
                jnp.where(lane < 2 * n_heads, mid, jnp.where(lane < 3 * n_heads, lo, zero)))

    def step(jb, carries, heads, q_aug, masked):
        start = pl.multiple_of(jb * blk, blk)
        aj = aug_ref[pl.ds(start, blk), :]
        out = []
        for h, qa, (m, l, acc) in zip(heads, q_aug, carries):
            hs = slice(h * HEAD_DIM, (h + 1) * HEAD_DIM)
            kj = jnp.concatenate([k_ref[pl.ds(start, blk), hs], aj], axis=1)
            s = _dot_nt(qa, kj)
            if masked:
                s = jnp.where(causal, s, MASK_VALUE)
            m_new = jnp.maximum(m, jnp.max(s, axis=-1, keepdims=True))
            alpha = jnp.exp2(m - m_new)
            p = jnp.exp2(s - m_new)
            l = alpha * l + jnp.sum(p, axis=-1, keepdims=True)
            acc = alpha * acc + _dot(p.astype(BF16), v_ref[pl.ds(start, blk), hs])
            out.append((m_new, l, acc))
        return tuple(out)

    for h0 in range(0, n_heads, HEAD_GROUP):
        heads = tuple(range(h0, h0 + HEAD_GROUP))
        q_aug = []
        for h in heads:
            ones = jnp.where((lane % n_heads == h) & (lane < 3 * n_heads), 1.0, 0.0).astype(BF16)
            q_aug.append(jnp.concatenate([q_ref[:, h * HEAD_DIM:(h + 1) * HEAD_DIM], ones], axis=1))
        init = tuple((jnp.full((blk, 1), MASK_VALUE, F32), jnp.zeros((blk, 1), F32),
                      jnp.zeros((blk, HEAD_DIM), F32)) for _ in heads)
        carries = lax.fori_loop(0, i, lambda jb, c: step(jb, c, heads, q_aug, False), init)
        carries = step(i, carries, heads, q_aug, True)
        for h, (m, l, acc) in zip(heads, carries):
            o_ref[:, h * HEAD_DIM:(h + 1) * HEAD_DIM] = acc / l

    za_ref[...] = _rms(o_ref[...], ga_ref[...]).astype(BF16)


def _attn_prompt(q, kb, vb, lfrep, ga, seq, n_heads):
    t, c = q.shape
    b = t // seq
    blk = ATTN_BLOCK
    nq = seq // blk
    kern = functools.partial(_attn_prompt_kernel, blk=blk, n_heads=n_heads)
    return pl.pallas_call(
        kern,
        grid=(b, nq),
        in_specs=[
            pl.BlockSpec((blk, c), lambda bi, i: (bi * nq + i, 0)),
            pl.BlockSpec((seq, c), lambda bi, i: (bi, 0)),
            pl.BlockSpec((seq, c), lambda bi, i: (bi, 0)),
            pl.BlockSpec((seq, LANES), lambda bi, i: (bi, 0)),
            pl.BlockSpec((1, c), lambda bi, i: (0, 0)),
        ],
        out_specs=pl.BlockSpec((blk, c), lambda bi, i: (bi * nq + i, 0)),
        out_shape=jax.ShapeDtypeStruct((t, c), BF16),
        scratch_shapes=[pltpu.VMEM((seq, LANES), BF16), pltpu.VMEM((blk, c), F32)],
        compiler_params=_params("arbitrary", "arbitrary"),
        name="fox_prompt",
    )(q, kb, vb, lfrep, ga)


def _split3_dot(x, w):
    hi, mid, lo = _split3(x)
    return _dot(hi, w) + _dot(mid, w) + _dot(lo, w)


def _select_sum_kernel(x_ref, w_ref, o_ref):
    o_ref[...] = _split3_dot(x_ref[...], w_ref[...])


def _select_sum(x, w, tm):
    rows, k = x.shape
    n = w.shape[1]
    return pl.pallas_call(
        _select_sum_kernel,
        grid=(rows // tm,),
        in_specs=[pl.BlockSpec((tm, k), lambda i: (i, 0)), pl.BlockSpec((k, n), lambda i: (0, 0))],
        out_specs=pl.BlockSpec((tm, n), lambda i: (i, 0)),
        out_shape=jax.ShapeDtypeStruct((rows, n), F32),
        compiler_params=_params("parallel"),
        name="head_major_prefix_sum",
    )(x, w)


def _prefix_matrix(n_steps, n_heads, width):
    r = jnp.arange(n_steps * n_heads)
    cidx = jnp.arange(n_heads * width)
    t, h = r // n_heads, r % n_heads
    h2, t2 = cidx // width, cidx % width
    w = (h[:, None] == h2[None, :]) & (t[:, None] <= t2[None, :]) & (t2[None, :] < n_steps)
    return w.astype(BF16)


def _attn_sample_kernel(pt_ref, q_ref, kn_ref, vn_ref, fn_ref, ga_ref, ck_hbm, cv_hbm, pf_hbm,
                        za_ref, kbuf_ref, vbuf_ref, pfbuf_ref, sem, qbd_ref, kb_ref, vb_ref,
                        m_ref, l_ref, acc_ref, run_ref, *, pages, groups, n_steps, layer, n_heads,
                        dec_seq):
    g = pl.program_id(1)
    step = pl.program_id(0) * groups + g
    slot = step % 2

    def page_copies(step_idx, slot_idx):
        bb = step_idx // groups
        first = (step_idx % groups) * pages
        out = []
        for r in range(pages):
            page = pt_ref[bb, first + r]
            out.append(pltpu.make_async_copy(ck_hbm.at[layer, page], kbuf_ref.at[slot_idx, r],
                                             sem.at[slot_idx, 0]))
            out.append(pltpu.make_async_copy(cv_hbm.at[layer, page], vbuf_ref.at[slot_idx, r],
                                             sem.at[slot_idx, 1]))
            out.append(pltpu.make_async_copy(pf_hbm.at[page], pfbuf_ref.at[slot_idx, r],
                                             sem.at[slot_idx, 2]))
        return out

    @pl.when(step == 0)
    def _():
        for cp in page_copies(0, 0):
            cp.start()

    for cp in page_copies(step, slot):
        cp.wait()

    rows = dec_seq * n_heads
    c = n_heads * HEAD_DIM
    head_of_row = lax.broadcasted_iota(jnp.int32, (n_heads, c), 0)
    head_of_col = lax.broadcasted_iota(jnp.int32, (n_heads, c), 1) // HEAD_DIM
    diag = head_of_row == head_of_col

    @pl.when(g == 0)
    def _():
        for t in range(dec_seq):
            qt = jnp.broadcast_to(q_ref[0, t:t + 1, :], (n_heads, c))
            qbd_ref[t * n_heads:(t + 1) * n_heads, :] = jnp.where(diag, qt, 0.0).astype(BF16)
        m_ref[...] = jnp.full(m_ref.shape, MASK_VALUE, F32)
        l_ref[...] = jnp.zeros(l_ref.shape, F32)
        acc_ref[...] = jnp.zeros(acc_ref.shape, F32)
        run_ref[...] = jnp.zeros(run_ref.shape, F32)

    def online_update(s, v_bf):
        m_old = m_ref[...]
        m_new = jnp.maximum(m_old, jnp.max(s, axis=-1, keepdims=True))
        alpha = jnp.exp(m_old - m_new)
        p = jnp.exp(s - m_new)
        l_ref[...] = alpha * l_ref[...] + jnp.sum(p, axis=-1, keepdims=True)
        acc_ref[...] = alpha * acc_ref[...] + _dot(p.astype(BF16), v_bf)
        m_ref[...] = m_new

    run = run_ref[...]
    bias_parts = []
    for pg in range(pages):
        toks = slice(pg * PAGE_SIZE, (pg + 1) * PAGE_SIZE)
        for h in range(n_heads):
            hs = slice(h * HEAD_DIM, (h + 1) * HEAD_DIM)
            head_rows = pl.ds(h, PAGE_SIZE, stride=n_heads)
            kb_ref[toks, hs] = kbuf_ref[slot, pg, head_rows, :].astype(BF16)
            vb_ref[toks, hs] = vbuf_ref[slot, pg, head_rows, :].astype(BF16)
        pf = pfbuf_ref[slot, pg]
        bias_parts.append(run + pf)
        run = run + jnp.broadcast_to(pf[:, PAGE_SIZE - 1:PAGE_SIZE], pf.shape)
    run_ref[...] = run

    for cp in page_copies(jnp.minimum(step + 1, n_steps - 1), 1 - slot):
        cp.start()
    bias = jnp.concatenate(bias_parts, axis=1)
    bias = jnp.concatenate([bias] * dec_seq, axis=0)
    s = _dot_nt(qbd_ref[...], kb_ref[...]) - bias
    online_update(s, vb_ref[...])

    @pl.when(g == pl.num_programs(1) - 1)
    def _():
        pad = jnp.zeros((PAGE_SIZE - SUBLANES, c), F32)
        kn = jnp.concatenate([kn_ref[0], pad], axis=0).astype(BF16)
        vn = jnp.concatenate([vn_ref[0], pad], axis=0).astype(BF16)
        bias_n = jnp.concatenate([run + fn_ref[0]] * dec_seq, axis=0)
        s_n = _dot_nt(qbd_ref[...], kn) - bias_n
        q_step = lax.broadcasted_iota(jnp.int32, (rows, PAGE_SIZE), 0) // n_heads
        k_step = lax.broadcasted_iota(jnp.int32, (rows, PAGE_SIZE), 1)
        s_n = jnp.where(k_step <= q_step, s_n, MASK_VALUE)
        online_update(s_n, vn)
        o = acc_ref[...] / l_ref[...]
        for t in range(dec_seq):
            ot = jnp.where(diag, o[t * n_heads:(t + 1) * n_heads, :], 0.0)
            ot = jnp.sum(ot, axis=0, keepdims=True)
            za_ref[0, t:t + 1, :] = _rms(ot, ga_ref[...])

    @pl.when(step == n_steps - 1)
    def _():
        for cp in page_copies(n_steps - 1, 1 - slot):
            cp.wait()


def _attn_sample(page_table, q, k_new, v_new, f_new, ga, cache_k, cache_v, pfx, layer, n_heads,
                 dec_seq):
    nb, n_pages = page_table.shape
    c = n_heads * HEAD_DIM
    pages = PAGES_PER_STEP
    groups = n_pages // pages
    rows = dec_seq * n_heads
    per_b = lambda b, g, pt: (b, 0, 0)
    kern = functools.partial(_attn_sample_kernel, pages=pages, groups=groups, n_steps=nb * groups,
                             layer=layer, n_heads=n_heads, dec_seq=dec_seq)
    grid_spec = pltpu.PrefetchScalarGridSpec(
        num_scalar_prefetch=1,
        grid=(nb, groups),
        in_specs=[
            pl.BlockSpec((1, dec_seq, c), per_b),
            pl.BlockSpec((1, SUBLANES, c), per_b),
            pl.BlockSpec((1, SUBLANES, c), per_b),
            pl.BlockSpec((1, n_heads, LANES), per_b),
            pl.BlockSpec((1, c), lambda b, g, pt: (0, 0)),
            pl.BlockSpec(memory_space=pl.ANY),
            pl.BlockSpec(memory_space=pl.ANY),
            pl.BlockSpec(memory_space=pl.ANY),
        ],
        out_specs=pl.BlockSpec((1, dec_seq, c), per_b),
        scratch_shapes=[
            pltpu.VMEM((2, pages, PAGE_SIZE * n_heads, HEAD_DIM), F32),
            pltpu.VMEM((2, pages, PAGE_SIZE * n_heads, HEAD_DIM), F32),
            pltpu.VMEM((2, pages, n_heads, LANES), F32),
            pltpu.SemaphoreType.DMA((2, 3)),
            pltpu.VMEM((rows, c), BF16),
            pltpu.VMEM((pages * PAGE_SIZE, c), BF16),
            pltpu.VMEM((pages * PAGE_SIZE, c), BF16),
            pltpu.VMEM((rows, 1), F32),
            pltpu.VMEM((rows, 1), F32),
            pltpu.VMEM((rows, c), F32),
            pltpu.VMEM((n_heads, LANES), F32),
        ],
    )
    return pl.pallas_call(
        kern,
        grid_spec=grid_spec,
        out_shape=jax.ShapeDtypeStruct((nb, dec_seq, c), F32),
        compiler_params=_params("arbitrary", "arbitrary"),
        name="fox_sample",
    )(page_table, q, k_new, v_new, f_new, ga, cache_k, cache_v, pfx)


def _resident(block_shape, index_map):
    return pl.BlockSpec(block_shape, index_map, pipeline_mode=pl.Buffered(1))


def _mixout_kernel(h_ref, zc_ref, za_ref, wc_ref, wa_ref, o_ref, wcb_ref, wab_ref):
    @pl.when(pl.program_id(0) == 0)
    def _():
        wcb_ref[...] = wc_ref[...].astype(BF16)
        wab_ref[...] = wa_ref[...].astype(BF16)

    o_ref[...] = (h_ref[...] + _dot(zc_ref[...].astype(BF16), wcb_ref[...])
                  + _dot(za_ref[...].astype(BF16), wab_ref[...]))


def _mixout(h, zc, za, w_out, layer, tm):
    t, d = h.shape
    c = zc.shape[1]
    assert za.shape[1] == c and w_out.shape[1] == 2 * c
    row = lambda i: (i, 0)
    return pl.pallas_call(
        _mixout_kernel,
        grid=(t // tm,),
        in_specs=[pl.BlockSpec((tm, d), row), pl.BlockSpec((tm, c), row), pl.BlockSpec((tm, c), row),
                  _resident((None, c, d), lambda i: (layer, 0, 0)),
                  _resident((None, c, d), lambda i: (layer, 1, 0))],
        out_specs=pl.BlockSpec((tm, d), row),
        out_shape=jax.ShapeDtypeStruct((t, d), F32),
        scratch_shapes=[pltpu.VMEM((c, d), BF16), pltpu.VMEM((c, d), BF16)],
        compiler_params=_params("arbitrary"),
        name="mixer_out",
    )(h, zc, za, w_out, w_out)


def _ple_kernel(h_ref, p_ref, g_ref, wg_ref, wp_ref, gf_ref, o_ref, wgb_ref, wpb_ref, *, final):
    @pl.when(pl.program_id(0) == 0)
    def _():
        wgb_ref[...] = wg_ref[...].astype(BF16)
        wpb_ref[...] = wp_ref[...].astype(BF16)

    h = h_ref[...]
    hn = _rms(h, g_ref[...]).astype(BF16)
    gate = jax.nn.sigmoid(_dot(hn, wgb_ref[...]))
    out = h + gate * _dot(p_ref[...].astype(BF16), wpb_ref[...])
    o_ref[...] = _rms(out, gf_ref[...]) if final else out


def _ple(h, p, g, wg, wp, gf, layer, final, tm):
    t, d = h.shape
    pd = p.shape[1]
    row = lambda i: (i, 0)
    fixed = lambda i: (0, 0)
    return pl.pallas_call(
        functools.partial(_ple_kernel, final=final),
        grid=(t // tm,),
        in_specs=[pl.BlockSpec((tm, d), row), pl.BlockSpec((tm, pd), row), pl.BlockSpec((1, d), fixed),
                  _resident((None, d, d), lambda i: (layer, 0, 0)),
                  _resident((None, pd, d), lambda i: (layer, 0, 0)),
                  pl.BlockSpec((1, d), fixed)],
        out_specs=pl.BlockSpec((tm, d), row),
        out_shape=jax.ShapeDtypeStruct((t, d), F32),
        scratch_shapes=[pltpu.VMEM((d, d), BF16), pltpu.VMEM((pd, d), BF16)],
        compiler_params=_params("arbitrary"),
        name="ple",
    )(h, p, g, wg, wp, gf)


def kernel(x_prompt, x_sample, p_prompt, p_sample, cache_k, cache_v, cache_logf, state_conv, page_table, norm_ffn1, w_ffn1_gate, w_ffn1_up, w_ffn1_down, norm_mix, w_in, b_f, conv_w, norm_conv_out, norm_attn_out, w_out, norm_ffn2, w_ffn2_gate, w_ffn2_up, w_ffn2_down, norm_ple, w_ple_gate, w_ple_proj, norm_final):
    batch, seq, d = x_prompt.shape
    dec_batch, dec_seq, _ = x_sample.shape
    depth = w_in.shape[0]
    n_heads = b_f.shape[1]
    conv_dim = conv_w.shape[2]
    attn_dim = n_heads * HEAD_DIM
    n_pool = cache_k.shape[1]
    scale = HEAD_DIM ** -0.5
    n_main = 3 * conv_dim + 3 * attn_dim
    assert conv_dim == attn_dim and dec_seq >= CONV_W - 1 and dec_seq <= SUBLANES
    tp, ts = batch * seq, dec_batch * dec_seq

    hp = x_prompt.reshape(tp, d)
    hs = x_sample.reshape(ts, d)
    row2 = lambda a: a.reshape(1, -1)
    page_prefix_w = _prefix_matrix(PAGE_SIZE, n_heads, PAGE_SIZE)
    new_prefix_w = _prefix_matrix(dec_seq, n_heads, LANES)

    outs = [[] for _ in range(8)]
    for l in range(depth):
        w_main = w_in[l].astype(BF16)
        w_f = jnp.tile(w_in[l, :, n_main:], (1, LANES // n_heads)).astype(BF16)
        g1, gm, g2, gp = row2(norm_ffn1[l]), row2(norm_mix[l]), row2(norm_ffn2[l]), row2(norm_ple[l])
        gc, ga = row2(norm_conv_out[l]), row2(norm_attn_out[l])
        bf = row2(jnp.tile(b_f[l], LANES // n_heads))
        cw = conv_w[l]
        last = l == depth - 1
        gfin = row2(norm_final)

        hp, hs = _ffn(hp, hs, g1, w_ffn1_gate, w_ffn1_up, w_ffn1_down, l)

        zc, q, k, v, kb, vb, lf, lfrep, cs = _mixin_prompt(hp, gm, w_main, w_f, bf, cw, gc, seq,
                                                           n_heads, scale * LOG2E)
        za = _attn_prompt(q, kb, vb, lfrep, ga, seq, n_heads)
        hp = _mixout(hp, zc, za, w_out, l, TOKEN_TILE)
        outs[0].append(k.reshape(batch, seq, n_heads, HEAD_DIM))
        outs[1].append(v.reshape(batch, seq, n_heads, HEAD_DIM))
        outs[2].append(lf.reshape(batch, seq, n_heads))
        outs[3].append(cs)

        st = state_conv[l]
        zero = jnp.zeros((dec_batch, dec_seq - 1, conv_dim), F32)
        s1 = jnp.concatenate([st[:, 1:2], zero], axis=1).reshape(ts, conv_dim)
        s2 = jnp.concatenate([st, zero[:, 1:]], axis=1).reshape(ts, conv_dim)
        zc, q, k, v, u, lf = _mixin_sample(hs, gm, w_main, w_f, bf, cw, gc, s1, s2,
                                           dec_seq, n_heads, scale)
        pfx = _select_sum(cache_logf[l].reshape(n_pool, PAGE_SIZE * n_heads), page_prefix_w, 512)
        pfx = pfx.reshape(n_pool, n_heads, PAGE_SIZE)
        f_new = _select_sum(lf.reshape(dec_batch, dec_seq * n_heads), new_prefix_w, dec_batch)
        f_new = f_new.reshape(dec_batch, n_heads, LANES)
        pad8 = lambda a: jnp.pad(a.reshape(dec_batch, dec_seq, attn_dim),
                                 ((0, 0), (0, SUBLANES - dec_seq), (0, 0)))
        za = _attn_sample(page_table, q.reshape(dec_batch, dec_seq, attn_dim), pad8(k), pad8(v),
                          f_new, ga,
                          cache_k.reshape(depth, n_pool, PAGE_SIZE * n_heads, HEAD_DIM),
                          cache_v.reshape(depth, n_pool, PAGE_SIZE * n_heads, HEAD_DIM),
                          pfx, l, n_heads, dec_seq)
        hs = _mixout(hs, zc, za.reshape(ts, attn_dim), w_out, l, ts)

        hp, hs = _ffn(hp, hs, g2, w_ffn2_gate, w_ffn2_up, w_ffn2_down, l)
        hp = _ple(hp, p_prompt[l].reshape(tp, -1), gp, w_ple_gate, w_ple_proj, gfin, l, last,
                  TOKEN_TILE)
        hs = _ple(hs, p_sample[l].reshape(ts, -1), gp, w_ple_gate, w_ple_proj, gfin, l, last, ts)
        outs[4].append(k.reshape(dec_batch, dec_seq, n_heads, HEAD_DIM))
        outs[5].append(v.reshape(dec_batch, dec_seq, n_heads, HEAD_DIM))
        outs[6].append(lf.reshape(dec_batch, dec_seq, n_heads))
        outs[7].append(u.reshape(dec_batch, dec_seq, conv_dim)[:, dec_seq - (CONV_W - 1):])

    return (hp.reshape(batch, seq, d), hs.reshape(dec_batch, dec_seq, d),
            *(jnp.stack(o) for o in outs))
```

```python
import functools

import jax
import jax.numpy as jnp
from jax import lax
from jax.experimental import pallas as pl
from jax.experimental.pallas import tpu as pltpu

F32 = jnp.float32
BF16 = jnp.bfloat16

EPS = 1e-6
HEAD_DIM = 128
PAGE_SIZE = 128
CONV_W = 3
LANES = 128
SUBLANES = 8
MASK_VALUE = -1e30
VMEM_LIMIT = 56 * 1024 * 1024

TOKEN_TILE = 512
FFN_TOKEN_TILE = 1024
FFN_FF_TILE = 256
ATTN_BLOCK = 256
HEAD_GROUP = 8
LOG2E = 1.4426950408889634
PAGES_PER_STEP = 16


def _params(*sem):
    return pltpu.CompilerParams(dimension_semantics=sem, vmem_limit_bytes=VMEM_LIMIT)


def _rms(x, g):
    ms = jnp.mean(x * x, axis=-1, keepdims=True)
    return x * lax.rsqrt(ms + EPS) * g


def _dot(a, b):
    return jnp.dot(a, b, preferred_element_type=F32)


def _dot_nt(a, b):
    return lax.dot_general(a, b, (((1,), (1,)), ((), ())), preferred_element_type=F32)


def _log_sigmoid(x):
    return jnp.minimum(x, 0.0) - jnp.log1p(jnp.exp(-jnp.abs(x)))


def _ffn_kernel(x_hbm, xs_ref, g_ref, wg_ref, wu_ref, wd_ref, o_ref, os_ref,
                xbuf_ref, hn_ref, hns_ref, sem, *, tm):
    i = pl.program_id(0)
    j = pl.program_id(1)
    last_tile = pl.num_programs(0) - 1

    def x_copy(tile):
        return pltpu.make_async_copy(x_hbm.at[pl.ds(tile * tm, tm), :], xbuf_ref, sem)

    @pl.when((i == 0) & (j == 0))
    def _():
        x_copy(0).start()

    @pl.when(j == 0)
    def _():
        x_copy(i).wait()
        x = xbuf_ref[...]
        hn_ref[...] = _rms(x, g_ref[...]).astype(BF16)
        o_ref[...] = x

    @pl.when((j == 1) & (i < last_tile))
    def _():
        x_copy(i + 1).start()

    wg = wg_ref[...].astype(BF16)
    wu = wu_ref[...].astype(BF16)
    wd = wd_ref[...].astype(BF16)

    def half_step(hn):
        gate = _dot(hn, wg)
        up = _dot(hn, wu)
        act = (gate * jax.nn.sigmoid(gate) * (0.5 * up)).astype(BF16)
        return _dot(act, wd)

    o_ref[...] += half_step(hn_ref[...])

    @pl.when(i == last_tile)
    def _():
        @pl.when(j == 0)
        def _():
            xs = xs_ref[...]
            hns_ref[...] = _rms(xs, g_ref[...]).astype(BF16)
            os_ref[...] = xs

        os_ref[...] += half_step(hns_ref[...])


def _ffn(x, xs, g, wg, wu, wd, layer):
    t, d = x.shape
    ts = xs.shape[0]
    f = wg.shape[2]
    tm, tf = FFN_TOKEN_TILE, FFN_FF_TILE
    assert f % tf == 0 and f // tf >= 2 and t % tm == 0
    return pl.pallas_call(
        functools.partial(_ffn_kernel, tm=tm),
        grid=(t // tm, f // tf),
        in_specs=[
            pl.BlockSpec(memory_space=pl.ANY),
            pl.BlockSpec((ts, d), lambda i, j: (0, 0)),
            pl.BlockSpec((1, d), lambda i, j: (0, 0)),
            pl.BlockSpec((None, d, tf), lambda i, j: (layer, 0, j)),
            pl.BlockSpec((None, d, tf), lambda i, j: (layer, 0, j)),
            pl.BlockSpec((None, tf, d), lambda i, j: (layer, j, 0)),
        ],
        out_specs=[pl.BlockSpec((tm, d), lambda i, j: (i, 0)),
                   pl.BlockSpec((ts, d), lambda i, j: (0, 0))],
        out_shape=[jax.ShapeDtypeStruct((t, d), F32), jax.ShapeDtypeStruct((ts, d), F32)],
        scratch_shapes=[pltpu.VMEM((tm, d), F32), pltpu.VMEM((tm, d), BF16),
                        pltpu.VMEM((ts, d), BF16), pltpu.SemaphoreType.DMA(())],
        compiler_params=_params("arbitrary", "arbitrary"),
        name="ffn_half",
    )(x, xs, g, wg, wu, wd)


def _mixin_common(j, x_ref, g_ref, w_ref, wf_ref, bf_ref, lf_ref, lfrep_ref, hn_ref, cb_ref,
                  n_heads):
    @pl.when(j == 0)
    def _():
        hn = _rms(x_ref[...], g_ref[...]).astype(BF16)
        hn_ref[...] = hn
        cb_ref[...] = _dot(hn, w_ref[...])
        lf = _log_sigmoid(_dot(hn, wf_ref[...]) + bf_ref[...])
        lf_ref[...] = lf[:, :n_heads]
        if lfrep_ref is not None:
            lfrep_ref[...] = lf

    return lambda: _dot(hn_ref[...], w_ref[...])


def _conv_taps(u, uext_ref, tm):
    uext_ref[pl.ds(SUBLANES, tm), :] = u
    um1 = uext_ref[pl.ds(SUBLANES - 1, tm), :]
    um2 = uext_ref[pl.ds(SUBLANES - 2, tm), :]
    return um1, um2


def _mixin_prompt_kernel(x_ref, g_ref, w_ref, wf_ref, bf_ref, cw_ref, gc_ref,
                         zc_ref, q_ref, k_ref, v_ref, kb_ref, vb_ref, lf_ref, lfrep_ref, cs_ref,
                         hn_ref, cb_ref, cc_ref, uext_ref, *, tm, tiles_per_seq, n_heads, scale):
    i = pl.program_id(0)
    j = pl.program_id(1)
    proj = _mixin_common(j, x_ref, g_ref, w_ref, wf_ref, bf_ref, lf_ref, lfrep_ref, hn_ref, cb_ref,
                         n_heads)

    @pl.when(j == 1)
    def _():
        cc_ref[...] = proj()

    @pl.when(j == 2)
    def _():
        @pl.when(i % tiles_per_seq == 0)
        def _():
            uext_ref[pl.ds(0, SUBLANES), :] = jnp.zeros((SUBLANES, cc_ref.shape[1]), F32)

        u = cc_ref[...] * proj()
        um1, um2 = _conv_taps(u, uext_ref, tm)
        y = cb_ref[...] * (cw_ref[0:1, :] * um2 + cw_ref[1:2, :] * um1 + cw_ref[2:3, :] * u)
        zc_ref[...] = _rms(y, gc_ref[...]).astype(BF16)
        cs_ref[0] = u[tm - (CONV_W - 1):, :]
        uext_ref[pl.ds(0, SUBLANES), :] = u[tm - SUBLANES:, :]

    @pl.when(j == 3)
    def _():
        q_ref[...] = (proj() * scale).astype(BF16)

    @pl.when(j == 4)
    def _():
        z = proj()
        k_ref[...] = z
        kb_ref[...] = z.astype(BF16)

    @pl.when(j == 5)
    def _():
        z = proj()
        v_ref[...] = z
        vb_ref[...] = z.astype(BF16)


def _mixin_prompt(x, g, w_main, w_f, b_f, cw, gc, seq, n_heads, scale):
    t, d = x.shape
    c = cw.shape[1]
    tm = TOKEN_TILE
    row = lambda i, j: (i, 0)
    fixed = lambda i, j: (0, 0)
    kern = functools.partial(_mixin_prompt_kernel, tm=tm, tiles_per_seq=seq // tm,
                             n_heads=n_heads, scale=scale)
    return pl.pallas_call(
        kern,
        grid=(t // tm, 6),
        in_specs=[
            pl.BlockSpec((tm, d), row),
            pl.BlockSpec((1, d), fixed),
            pl.BlockSpec((d, c), lambda i, j: (0, j)),
            pl.BlockSpec((d, LANES), fixed),
            pl.BlockSpec((1, LANES), fixed),
            pl.BlockSpec((CONV_W, c), fixed),
            pl.BlockSpec((1, c), fixed),
        ],
        out_specs=[
            pl.BlockSpec((tm, c), row),
            pl.BlockSpec((tm, c), row),
            pl.BlockSpec((tm, c), row),
            pl.BlockSpec((tm, c), row),
            pl.BlockSpec((tm, c), row),
            pl.BlockSpec((tm, c), row),
            pl.BlockSpec((tm, n_heads), row),
            pl.BlockSpec((tm, LANES), row),
            pl.BlockSpec((1, CONV_W - 1, c), lambda i, j: (i // (seq // tm), 0, 0)),
        ],
        out_shape=[
            jax.ShapeDtypeStruct((t, c), BF16),
            jax.ShapeDtypeStruct((t, c), BF16),
            jax.ShapeDtypeStruct((t, c), F32),
            jax.ShapeDtypeStruct((t, c), F32),
            jax.ShapeDtypeStruct((t, c), BF16),
            jax.ShapeDtypeStruct((t, c), BF16),
            jax.ShapeDtypeStruct((t, n_heads), F32),
            jax.ShapeDtypeStruct((t, LANES), F32),
            jax.ShapeDtypeStruct((t // seq, CONV_W - 1, c), F32),
        ],
        scratch_shapes=[
            pltpu.VMEM((tm, d), BF16),
            pltpu.VMEM((tm, c), F32),
            pltpu.VMEM((tm, c), F32),
            pltpu.VMEM((tm + SUBLANES, c), F32),
        ],
        compiler_params=_params("arbitrary", "arbitrary"),
        name="mixer_in_prompt",
    )(x, g, w_main, w_f, b_f, cw, gc)


def _mixin_sample_kernel(x_ref, g_ref, w_ref, wf_ref, bf_ref, cw_ref, gc_ref, s1_ref, s2_ref,
                         zc_ref, q_ref, k_ref, v_ref, u_ref, lf_ref,
                         hn_ref, cb_ref, cc_ref, uext_ref, *, tm, dec_seq, n_heads, scale):
    j = pl.program_id(1)
    proj = _mixin_common(j, x_ref, g_ref, w_ref, wf_ref, bf_ref, lf_ref, None, hn_ref, cb_ref,
                         n_heads)

    @pl.when(j == 1)
    def _():
        cc_ref[...] = proj()

    @pl.when(j == 2)
    def _():
        uext_ref[pl.ds(0, SUBLANES), :] = jnp.zeros((SUBLANES, cc_ref.shape[1]), F32)
        u = cc_ref[...] * proj()
        um1, um2 = _conv_taps(u, uext_ref, tm)
        step = lax.broadcasted_iota(jnp.int32, u.shape, 0) % dec_seq
        um1 = jnp.where(step >= 1, um1, s1_ref[...])
        um2 = jnp.where(step >= 2, um2, s2_ref[...])
        y = cb_ref[...] * (cw_ref[0:1, :] * um2 + cw_ref[1:2, :] * um1 + cw_ref[2:3, :] * u)
        zc_ref[...] = _rms(y, gc_ref[...]).astype(BF16)
        u_ref[...] = u

    @pl.when(j == 3)
    def _():
        q_ref[...] = proj() * scale

    @pl.when(j == 4)
    def _():
        k_ref[...] = proj()

    @pl.when(j == 5)
    def _():
        v_ref[...] = proj()


def _mixin_sample(x, g, w_main, w_f, b_f, cw, gc, s1, s2, dec_seq, n_heads, scale):
    t, d = x.shape
    c = cw.shape[1]
    tm = t
    row = lambda i, j: (i, 0)
    fixed = lambda i, j: (0, 0)
    kern = functools.partial(_mixin_sample_kernel, tm=tm, dec_seq=dec_seq, n_heads=n_heads,
                             scale=scale)
    return pl.pallas_call(
        kern,
        grid=(1, 6),
        in_specs=[
            pl.BlockSpec((tm, d), row),
            pl.BlockSpec((1, d), fixed),
            pl.BlockSpec((d, c), lambda i, j: (0, j)),
            pl.BlockSpec((d, LANES), fixed),
            pl.BlockSpec((1, LANES), fixed),
            pl.BlockSpec((CONV_W, c), fixed),
            pl.BlockSpec((1, c), fixed),
            pl.BlockSpec((tm, c), row),
            pl.BlockSpec((tm, c), row),
        ],
        out_specs=[
            pl.BlockSpec((tm, c), row),
            pl.BlockSpec((tm, c), row),
            pl.BlockSpec((tm, c), row),
            pl.BlockSpec((tm, c), row),
            pl.BlockSpec((tm, c), row),
            pl.BlockSpec((tm, n_heads), row),
        ],
        out_shape=[
            jax.ShapeDtypeStruct((t, c), BF16),
            jax.ShapeDtypeStruct((t, c), F32),
            jax.ShapeDtypeStruct((t, c), F32),
            jax.ShapeDtypeStruct((t, c), F32),
            jax.ShapeDtypeStruct((t, c), F32),
            jax.ShapeDtypeStruct((t, n_heads), F32),
        ],
        scratch_shapes=[
            pltpu.VMEM((tm, d), BF16),
            pltpu.VMEM((tm, c), F32),
            pltpu.VMEM((tm, c), F32),
            pltpu.VMEM((tm + SUBLANES, c), F32),
        ],
        compiler_params=_params("arbitrary", "arbitrary"),
        name="mixer_in_sample",
    )(x, g, w_main, w_f, b_f, cw, gc, s1, s2)


def _split3(x):
    hi = x.astype(BF16)
    r1 = x - hi.astype(F32)
    mid = r1.astype(BF16)
    lo = (r1 - mid.astype(F32)).astype(BF16)
    return hi, mid, lo


def _attn_prompt_kernel(q_ref, k_ref, v_ref, lfrep_ref, ga_ref, za_ref, aug_ref, o_ref,
                        *, blk, n_heads):
    i = pl.program_id(1)
    seq = k_ref.shape[0]
    lane = lax.broadcasted_iota(jnp.int32, (blk, LANES), 1)
    row = lax.broadcasted_iota(jnp.int32, (blk, blk), 0)
    col = lax.broadcasted_iota(jnp.int32, (blk, blk), 1)
    causal = col <= row

    @pl.when(i == 0)
    def _():
        tri = jnp.where(causal, 1.0, 0.0).astype(BF16)
        carry = jnp.zeros((1, LANES), F32)
        for jb in range(seq // blk):
            rows = slice(jb * blk, (jb + 1) * blk)
            hi, mid, lo = _split3(lfrep_ref[rows, :])
            f = _dot(tri, hi) + _dot(tri, mid) + _dot(tri, lo) + carry
            carry = f[blk - 1:blk, :]
            hi, mid, lo = _split3(f * (-LOG2E))
            zero = jnp.zeros_like(hi)
            aug_ref[rows, :] = jnp.where(
                lane < n_heads, hi,
                jnp.where(lane < 2 * n_heads, mid, jnp.where(lane < 3 * n_heads, lo, zero)))

    def step(jb, carries, heads, q_aug, masked):
        start = pl.multiple_of(jb * blk, blk)
        aj = aug_ref[pl.ds(start, blk), :]
        out = []
        for h, qa, (m, l, acc) in zip(heads, q_aug, carries):
            hs = slice(h * HEAD_DIM, (h + 1) * HEAD_DIM)
            kj = jnp.concatenate([k_ref[pl.ds(start, blk), hs], aj], axis=1)
            s = _dot_nt(qa, kj)
            if masked:
                s = jnp.where(causal, s, MASK_VALUE)
            m_new = jnp.maximum(m, jnp.max(s, axis=-1, keepdims=True))
            alpha = jnp.exp2(m - m_new)
            p = jnp.exp2(s - m_new)
            l = alpha * l + jnp.sum(p, axis=-1, keepdims=True)
            acc = alpha * acc + _dot(p.astype(BF16), v_ref[pl.ds(start, blk), hs])
            out.append((m_new, l, acc))
        return tuple(out)

    for h0 in range(0, n_heads, HEAD_GROUP):
        heads = tuple(range(h0, h0 + HEAD_GROUP))
        q_aug = []
        for h in heads:
            ones = jnp.where((lane % n_heads == h) & (lane < 3 * n_heads), 1.0, 0.0).astype(BF16)
            q_aug.append(jnp.concatenate([q_ref[:, h * HEAD_DIM:(h + 1) * HEAD_DIM], ones], axis=1))
        init = tuple((jnp.full((blk, 1), MASK_VALUE, F32), jnp.zeros((blk, 1), F32),
                      jnp.zeros((blk, HEAD_DIM), F32)) for _ in heads)
        carries = lax.fori_loop(0, i, lambda jb, c: step(jb, c, heads, q_aug, False), init)
        carries = step(i, carries, heads, q_aug, True)
        for h, (m, l, acc) in zip(heads, carries):
            o_ref[:, h * HEAD_DIM:(h + 1) * HEAD_DIM] = acc / l

    za_ref[...] = _rms(o_ref[...], ga_ref[...]).astype(BF16)


def _attn_prompt(q, kb, vb, lfrep, ga, seq, n_heads):
    t, c = q.shape
    b = t // seq
    blk = ATTN_BLOCK
    nq = seq // blk
    kern = functools.partial(_attn_prompt_kernel, blk=blk, n_heads=n_heads)
    return pl.pallas_call(
        kern,
        grid=(b, nq),
        in_specs=[
            pl.BlockSpec((blk, c), lambda bi, i: (bi * nq + i, 0)),
            pl.BlockSpec((seq, c), lambda bi, i: (bi, 0)),
            pl.BlockSpec((seq, c), lambda bi, i: (bi, 0)),
            pl.BlockSpec((seq, LANES), lambda bi, i: (bi, 0)),
            pl.BlockSpec((1, c), lambda bi, i: (0, 0)),
        ],
        out_specs=pl.BlockSpec((blk, c), lambda bi, i: (bi * nq + i, 0)),
        out_shape=jax.ShapeDtypeStruct((t, c), BF16),
        scratch_shapes=[pltpu.VMEM((seq, LANES), BF16), pltpu.VMEM((blk, c), F32)],
        compiler_params=_params("arbitrary", "arbitrary"),
        name="fox_prompt",
    )(q, kb, vb, lfrep, ga)


def _split3_dot(x, w):
    hi, mid, lo = _split3(x)
    return _dot(hi, w) + _dot(mid, w) + _dot(lo, w)


def _select_sum_kernel(x_ref, w_ref, o_ref):
    o_ref[...] = _split3_dot(x_ref[...], w_ref[...])


def _select_sum(x, w, tm):
    rows, k = x.shape
    n = w.shape[1]
    return pl.pallas_call(
        _select_sum_kernel,
        grid=(rows // tm,),
        in_specs=[pl.BlockSpec((tm, k), lambda i: (i, 0)), pl.BlockSpec((k, n), lambda i: (0, 0))],
        out_specs=pl.BlockSpec((tm, n), lambda i: (i, 0)),
        out_shape=jax.ShapeDtypeStruct((rows, n), F32),
        compiler_params=_params("parallel"),
        name="head_major_prefix_sum",
    )(x, w)


def _prefix_matrix(n_steps, n_heads, width):
    r = jnp.arange(n_steps * n_heads)
    cidx = jnp.arange(n_heads * width)
    t, h = r // n_heads, r % n_heads
    h2, t2 = cidx // width, cidx % width
    w = (h[:, None] == h2[None, :]) & (t[:, None] <= t2[None, :]) & (t2[None, :] < n_steps)
    return w.astype(BF16)


def _attn_sample_kernel(pt_ref, q_ref, kn_ref, vn_ref, fn_ref, ga_ref, ck_hbm, cv_hbm, pf_hbm,
                        za_ref, kbuf_ref, vbuf_ref, pfbuf_ref, sem, qbd_ref, kb_ref, vb_ref,
                        m_ref, l_ref, acc_ref, run_ref, *, pages, groups, n_steps, layer, n_heads,
                        dec_seq):
    g = pl.program_id(1)
    step = pl.program_id(0) * groups + g
    slot = step % 2

    def page_copies(step_idx, slot_idx):
        bb = step_idx // groups
        first = (step_idx % groups) * pages
        out = []
        for r in range(pages):
            page = pt_ref[bb, first + r]
            out.append(pltpu.make_async_copy(ck_hbm.at[layer, page], kbuf_ref.at[slot_idx, r],
                                             sem.at[slot_idx, 0]))
            out.append(pltpu.make_async_copy(cv_hbm.at[layer, page], vbuf_ref.at[slot_idx, r],
                                             sem.at[slot_idx, 1]))
            out.append(pltpu.make_async_copy(pf_hbm.at[page], pfbuf_ref.at[slot_idx, r],
                                             sem.at[slot_idx, 2]))
        return out

    @pl.when(step == 0)
    def _():
        for cp in page_copies(0, 0):
            cp.start()

    for cp in page_copies(jnp.minimum(step + 1, n_steps - 1), 1 - slot):
        cp.start()

    for cp in page_copies(step, slot):
        cp.wait()

    rows = dec_seq * n_heads
    c = n_heads * HEAD_DIM
    head_of_row = lax.broadcasted_iota(jnp.int32, (n_heads, c), 0)
    head_of_col = lax.broadcasted_iota(jnp.int32, (n_heads, c), 1) // HEAD_DIM
    diag = head_of_row == head_of_col

    @pl.when(g == 0)
    def _():
        for t in range(dec_seq):
            qt = jnp.broadcast_to(q_ref[0, t:t + 1, :], (n_heads, c))
            qbd_ref[t * n_heads:(t + 1) * n_heads, :] = jnp.where(diag, qt, 0.0).astype(BF16)
        m_ref[...] = jnp.full(m_ref.shape, MASK_VALUE, F32)
        l_ref[...] = jnp.zeros(l_ref.shape, F32)
        acc_ref[...] = jnp.zeros(acc_ref.shape, F32)
        run_ref[...] = jnp.zeros(run_ref.shape, F32)

    def online_update(s, v_bf):
        m_old = m_ref[...]
        m_new = jnp.maximum(m_old, jnp.max(s, axis=-1, keepdims=True))
        alpha = jnp.exp(m_old - m_new)
        p = jnp.exp(s - m_new)
        l_ref[...] = alpha * l_ref[...] + jnp.sum(p, axis=-1, keepdims=True)
        acc_ref[...] = alpha * acc_ref[...] + _dot(p.astype(BF16), v_bf)
        m_ref[...] = m_new

    run = run_ref[...]
    bias_parts = []
    for pg in range(pages):
        toks = slice(pg * PAGE_SIZE, (pg + 1) * PAGE_SIZE)
        for h in range(n_heads):
            hs = slice(h * HEAD_DIM, (h + 1) * HEAD_DIM)
            head_rows = pl.ds(h, PAGE_SIZE, stride=n_heads)
            kb_ref[toks, hs] = kbuf_ref[slot, pg, head_rows, :].astype(BF16)
            vb_ref[toks, hs] = vbuf_ref[slot, pg, head_rows, :].astype(BF16)
        pf = pfbuf_ref[slot, pg]
        bias_parts.append(run + pf)
        run = run + jnp.broadcast_to(pf[:, PAGE_SIZE - 1:PAGE_SIZE], pf.shape)
    run_ref[...] = run
    bias = jnp.concatenate(bias_parts, axis=1)
    bias = jnp.concatenate([bias] * dec_seq, axis=0)
    s = _dot_nt(qbd_ref[...], kb_ref[...]) - bias
    online_update(s, vb_ref[...])

    @pl.when(g == pl.num_programs(1) - 1)
    def _():
        pad = jnp.zeros((PAGE_SIZE - SUBLANES, c), F32)
        kn = jnp.concatenate([kn_ref[0], pad], axis=0).astype(BF16)
        vn = jnp.concatenate([vn_ref[0], pad], axis=0).astype(BF16)
        bias_n = jnp.concatenate([run + fn_ref[0]] * dec_seq, axis=0)
        s_n = _dot_nt(qbd_ref[...], kn) - bias_n
        q_step = lax.broadcasted_iota(jnp.int32, (rows, PAGE_SIZE), 0) // n_heads
        k_step = lax.broadcasted_iota(jnp.int32, (rows, PAGE_SIZE), 1)
        s_n = jnp.where(k_step <= q_step, s_n, MASK_VALUE)
        online_update(s_n, vn)
        o = acc_ref[...] / l_ref[...]
        for t in range(dec_seq):
            ot = jnp.where(diag, o[t * n_heads:(t + 1) * n_heads, :], 0.0)
            ot = jnp.sum(ot, axis=0, keepdims=True)
            za_ref[0, t:t + 1, :] = _rms(ot, ga_ref[...])

    @pl.when(step == n_steps - 1)
    def _():
        for cp in page_copies(n_steps - 1, 1 - slot):
            cp.wait()


def _attn_sample(page_table, q, k_new, v_new, f_new, ga, cache_k, cache_v, pfx, layer, n_heads,
                 dec_seq):
    nb, n_pages = page_table.shape
    c = n_heads * HEAD_DIM
    pages = PAGES_PER_STEP
    groups = n_pages // pages
    rows = dec_seq * n_heads
    per_b = lambda b, g, pt: (b, 0, 0)
    kern = functools.partial(_attn_sample_kernel, pages=pages, groups=groups, n_steps=nb * groups,
                             layer=layer, n_heads=n_heads, dec_seq=dec_seq)
    grid_spec = pltpu.PrefetchScalarGridSpec(
        num_scalar_prefetch=1,
        grid=(nb, groups),
        in_specs=[
            pl.BlockSpec((1, dec_seq, c), per_b),
            pl.BlockSpec((1, SUBLANES, c), per_b),
            pl.BlockSpec((1, SUBLANES, c), per_b),
            pl.BlockSpec((1, n_heads, LANES), per_b),
            pl.BlockSpec((1, c), lambda b, g, pt: (0, 0)),
            pl.BlockSpec(memory_space=pl.ANY),
            pl.BlockSpec(memory_space=pl.ANY),
            pl.BlockSpec(memory_space=pl.ANY),
        ],
        out_specs=pl.BlockSpec((1, dec_seq, c), per_b),
        scratch_shapes=[
            pltpu.VMEM((2, pages, PAGE_SIZE * n_heads, HEAD_DIM), F32),
            pltpu.VMEM((2, pages, PAGE_SIZE * n_heads, HEAD_DIM), F32),
            pltpu.VMEM((2, pages, n_heads, LANES), F32),
            pltpu.SemaphoreType.DMA((2, 3)),
            pltpu.VMEM((rows, c), BF16),
            pltpu.VMEM((pages * PAGE_SIZE, c), BF16),
            pltpu.VMEM((pages * PAGE_SIZE, c), BF16),
            pltpu.VMEM((rows, 1), F32),
            pltpu.VMEM((rows, 1), F32),
            pltpu.VMEM((rows, c), F32),
            pltpu.VMEM((n_heads, LANES), F32),
        ],
    )
    return pl.pallas_call(
        kern,
        grid_spec=grid_spec,
        out_shape=jax.ShapeDtypeStruct((nb, dec_seq, c), F32),
        compiler_params=_params("arbitrary", "arbitrary"),
        name="fox_sample",
    )(page_table, q, k_new, v_new, f_new, ga, cache_k, cache_v, pfx)


def _resident(block_shape, index_map):
    return pl.BlockSpec(block_shape, index_map, pipeline_mode=pl.Buffered(1))


def _mixout_kernel(h_ref, zc_ref, za_ref, wc_ref, wa_ref, o_ref, wcb_ref, wab_ref):
    @pl.when(pl.program_id(0) == 0)
    def _():
        wcb_ref[...] = wc_ref[...].astype(BF16)
        wab_ref[...] = wa_ref[...].astype(BF16)

    o_ref[...] = (h_ref[...] + _dot(zc_ref[...].astype(BF16), wcb_ref[...])
                  + _dot(za_ref[...].astype(BF16), wab_ref[...]))


def _mixout(h, zc, za, w_out, layer, tm):
    t, d = h.shape
    c = zc.shape[1]
    assert za.shape[1] == c and w_out.shape[1] == 2 * c
    row = lambda i: (i, 0)
    return pl.pallas_call(
        _mixout_kernel,
        grid=(t // tm,),
        in_specs=[pl.BlockSpec((tm, d), row), pl.BlockSpec((tm, c), row), pl.BlockSpec((tm, c), row),
                  _resident((None, c, d), lambda i: (layer, 0, 0)),
                  _resident((None, c, d), lambda i: (layer, 1, 0))],
        out_specs=pl.BlockSpec((tm, d), row),
        out_shape=jax.ShapeDtypeStruct((t, d), F32),
        scratch_shapes=[pltpu.VMEM((c, d), BF16), pltpu.VMEM((c, d), BF16)],
        compiler_params=_params("arbitrary"),
        name="mixer_out",
    )(h, zc, za, w_out, w_out)


def _ple_kernel(h_ref, p_ref, g_ref, wg_ref, wp_ref, gf_ref, o_ref, wgb_ref, wpb_ref, *, final):
    @pl.when(pl.program_id(0) == 0)
    def _():
        wgb_ref[...] = wg_ref[...].astype(BF16)
        wpb_ref[...] = wp_ref[...].astype(BF16)

    h = h_ref[...]
    hn = _rms(h, g_ref[...]).astype(BF16)
    gate = jax.nn.sigmoid(_dot(hn, wgb_ref[...]))
    out = h + gate * _dot(p_ref[...].astype(BF16), wpb_ref[...])
    o_ref[...] = _rms(out, gf_ref[...]) if final else out


def _ple(h, p, g, wg, wp, gf, layer, final, tm):
    t, d = h.shape
    pd = p.shape[1]
    row = lambda i: (i, 0)
    fixed = lambda i: (0, 0)
    return pl.pallas_call(
        functools.partial(_ple_kernel, final=final),
        grid=(t // tm,),
        in_specs=[pl.BlockSpec((tm, d), row), pl.BlockSpec((tm, pd), row), pl.BlockSpec((1, d), fixed),
                  _resident((None, d, d), lambda i: (layer, 0, 0)),
                  _resident((None, pd, d), lambda i: (layer, 0, 0)),
                  pl.BlockSpec((1, d), fixed)],
        out_specs=pl.BlockSpec((tm, d), row),
        out_shape=jax.ShapeDtypeStruct((t, d), F32),
        scratch_shapes=[pltpu.VMEM((d, d), BF16), pltpu.VMEM((pd, d), BF16)],
        compiler_params=_params("arbitrary"),
        name="ple",
    )(h, p, g, wg, wp, gf)


def kernel(x_prompt, x_sample, p_prompt, p_sample, cache_k, cache_v, cache_logf, state_conv, page_table, norm_ffn1, w_ffn1_gate, w_ffn1_up, w_ffn1_down, norm_mix, w_in, b_f, conv_w, norm_conv_out, norm_attn_out, w_out, norm_ffn2, w_ffn2_gate, w_ffn2_up, w_ffn2_down, norm_ple, w_ple_gate, w_ple_proj, norm_final):
    batch, seq, d = x_prompt.shape
    dec_batch, dec_seq, _ = x_sample.shape
    depth = w_in.shape[0]
    n_heads = b_f.shape[1]
    conv_dim = conv_w.shape[2]
    attn_dim = n_heads * HEAD_DIM
    n_pool = cache_k.shape[1]
    scale = HEAD_DIM ** -0.5
    n_main = 3 * conv_dim + 3 * attn_dim
    assert conv_dim == attn_dim and dec_seq >= CONV_W - 1 and dec_seq <= SUBLANES
    tp, ts = batch * seq, dec_batch * dec_seq

    hp = x_prompt.reshape(tp, d)
    hs = x_sample.reshape(ts, d)
    row2 = lambda a: a.reshape(1, -1)
    page_prefix_w = _prefix_matrix(PAGE_SIZE, n_heads, PAGE_SIZE)
    new_prefix_w = _prefix_matrix(dec_seq, n_heads, LANES)

    outs = [[] for _ in range(8)]
    for l in range(depth):
        w_main = w_in[l].astype(BF16)
        w_f = jnp.tile(w_in[l, :, n_main:], (1, LANES // n_heads)).astype(BF16)
        g1, gm, g2, gp = row2(norm_ffn1[l]), row2(norm_mix[l]), row2(norm_ffn2[l]), row2(norm_ple[l])
        gc, ga = row2(norm_conv_out[l]), row2(norm_attn_out[l])
        bf = row2(jnp.tile(b_f[l], LANES // n_heads))
        cw = conv_w[l]
        last = l == depth - 1
        gfin = row2(norm_final)

        hp, hs = _ffn(hp, hs, g1, w_ffn1_gate, w_ffn1_up, w_ffn1_down, l)

        zc, q, k, v, kb, vb, lf, lfrep, cs = _mixin_prompt(hp, gm, w_main, w_f, bf, cw, gc, seq,
                                                           n_heads, scale * LOG2E)
        za = _attn_prompt(q, kb, vb, lfrep, ga, seq, n_heads)
        hp = _mixout(hp, zc, za, w_out, l, TOKEN_TILE)
        outs[0].append(k.reshape(batch, seq, n_heads, HEAD_DIM))
        outs[1].append(v.reshape(batch, seq, n_heads, HEAD_DIM))
        outs[2].append(lf.reshape(batch, seq, n_heads))
        outs[3].append(cs)

        st = state_conv[l]
        zero = jnp.zeros((dec_batch, dec_seq - 1, conv_dim), F32)
        s1 = jnp.concatenate([st[:, 1:2], zero], axis=1).reshape(ts, conv_dim)
        s2 = jnp.concatenate([st, zero[:, 1:]], axis=1).reshape(ts, conv_dim)
        zc, q, k, v, u, lf = _mixin_sample(hs, gm, w_main, w_f, bf, cw, gc, s1, s2,
                                           dec_seq, n_heads, scale)
        pfx = _select_sum(cache_logf[l].reshape(n_pool, PAGE_SIZE * n_heads), page_prefix_w, 512)
        pfx = pfx.reshape(n_pool, n_heads, PAGE_SIZE)
        f_new = _select_sum(lf.reshape(dec_batch, dec_seq * n_heads), new_prefix_w, dec_batch)
        f_new = f_new.reshape(dec_batch, n_heads, LANES)
        pad8 = lambda a: jnp.pad(a.reshape(dec_batch, dec_seq, attn_dim),
                                 ((0, 0), (0, SUBLANES - dec_seq), (0, 0)))
        za = _attn_sample(page_table, q.reshape(dec_batch, dec_seq, attn_dim), pad8(k), pad8(v),
                          f_new, ga,
                          cache_k.reshape(depth, n_pool, PAGE_SIZE * n_heads, HEAD_DIM),
                          cache_v.reshape(depth, n_pool, PAGE_SIZE * n_heads, HEAD_DIM),
                          pfx, l, n_heads, dec_seq)
        hs = _mixout(hs, zc, za.reshape(ts, attn_dim), w_out, l, ts)

        hp, hs = _ffn(hp, hs, g2, w_ffn2_gate, w_ffn2_up, w_ffn2_down, l)
        hp = _ple(hp, p_prompt[l].reshape(tp, -1), gp, w_ple_gate, w_ple_proj, gfin, l, last,
                  TOKEN_TILE)
        hs = _ple(hs, p_sample[l].reshape(ts, -1), gp, w_ple_gate, w_ple_proj, gfin, l, last, ts)
        outs[4].append(k.reshape(dec_batch, dec_seq, n_heads, HEAD_DIM))
        outs[5].append(v.reshape(dec_batch, dec_seq, n_heads, HEAD_DIM))
        outs[6].append(lf.reshape(dec_batch, dec_seq, n_heads))
        outs[7].append(u.reshape(dec_batch, dec_seq, conv_dim)[:, dec_seq - (CONV_W - 1):])

    return (hp.reshape(batch, seq, d), hs.reshape(dec_batch, dec_seq, d),
            *(jnp.stack(o) for o in outs))
```

```python
import functools

import jax
import jax.numpy as jnp
from jax import lax
from jax.experimental import pallas as pl
from jax.experimental.pallas import tpu as pltpu

F32 = jnp.float32
BF16 = jnp.bfloat16

EPS = 1e-6
HEAD_DIM = 128
PAGE_SIZE = 128
CONV_W = 3
LANES = 128
SUBLANES = 8
MASK_VALUE = -1e30
VMEM_LIMIT = 56 * 1024 * 1024

TOKEN_TILE = 512
FFN_TOKEN_TILE = 1024
FFN_FF_TILE = 256
ATTN_BLOCK = 256
LOG2E = 1.4426950408889634
PAGES_PER_STEP = 16


def _params(*sem):
    return pltpu.CompilerParams(dimension_semantics=sem, vmem_limit_bytes=VMEM_LIMIT)


def _rms(x, g):
    ms = jnp.mean(x * x, axis=-1, keepdims=True)
    return x * lax.rsqrt(ms + EPS) * g


def _dot(a, b):
    return jnp.dot(a, b, preferred_element_type=F32)


def _dot_nt(a, b):
    return lax.dot_general(a, b, (((1,), (1,)), ((), ())), preferred_element_type=F32)


def _log_sigmoid(x):
    return jnp.minimum(x, 0.0) - jnp.log1p(jnp.exp(-jnp.abs(x)))


def _ffn_kernel(x_hbm, xs_ref, g_ref, wg_ref, wu_ref, wd_ref, o_ref, os_ref,
                xbuf_ref, hn_ref, hns_ref, sem, *, tm):
    i = pl.program_id(0)
    j = pl.program_id(1)
    last_tile = pl.num_programs(0) - 1

    def x_copy(tile):
        return pltpu.make_async_copy(x_hbm.at[pl.ds(tile * tm, tm), :], xbuf_ref, sem)

    @pl.when((i == 0) & (j == 0))
    def _():
        x_copy(0).start()

    @pl.when(j == 0)
    def _():
        x_copy(i).wait()
        x = xbuf_ref[...]
        hn_ref[...] = _rms(x, g_ref[...]).astype(BF16)
        o_ref[...] = x

    @pl.when((j == 1) & (i < last_tile))
    def _():
        x_copy(i + 1).start()

    wg = wg_ref[...].astype(BF16)
    wu = wu_ref[...].astype(BF16)
    wd = wd_ref[...].astype(BF16)

    def half_step(hn):
        gate = _dot(hn, wg)
        up = _dot(hn, wu)
        act = (gate * jax.nn.sigmoid(gate) * (0.5 * up)).astype(BF16)
        return _dot(act, wd)

    o_ref[...] += half_step(hn_ref[...])

    @pl.when(i == last_tile)
    def _():
        @pl.when(j == 0)
        def _():
            xs = xs_ref[...]
            hns_ref[...] = _rms(xs, g_ref[...]).astype(BF16)
            os_ref[...] = xs

        os_ref[...] += half_step(hns_ref[...])


def _ffn(x, xs, g, wg, wu, wd, layer):
    t, d = x.shape
    ts = xs.shape[0]
    f = wg.shape[2]
    tm, tf = FFN_TOKEN_TILE, FFN_FF_TILE
    assert f % tf == 0 and f // tf >= 2 and t % tm == 0
    return pl.pallas_call(
        functools.partial(_ffn_kernel, tm=tm),
        grid=(t // tm, f // tf),
        in_specs=[
            pl.BlockSpec(memory_space=pl.ANY),
            pl.BlockSpec((ts, d), lambda i, j: (0, 0)),
            pl.BlockSpec((1, d), lambda i, j: (0, 0)),
            pl.BlockSpec((None, d, tf), lambda i, j: (layer, 0, j)),
            pl.BlockSpec((None, d, tf), lambda i, j: (layer, 0, j)),
            pl.BlockSpec((None, tf, d), lambda i, j: (layer, j, 0)),
        ],
        out_specs=[pl.BlockSpec((tm, d), lambda i, j: (i, 0)),
                   pl.BlockSpec((ts, d), lambda i, j: (0, 0))],
        out_shape=[jax.ShapeDtypeStruct((t, d), F32), jax.ShapeDtypeStruct((ts, d), F32)],
        scratch_shapes=[pltpu.VMEM((tm, d), F32), pltpu.VMEM((tm, d), BF16),
                        pltpu.VMEM((ts, d), BF16), pltpu.SemaphoreType.DMA(())],
        compiler_params=_params("arbitrary", "arbitrary"),
        name="ffn_half",
    )(x, xs, g, wg, wu, wd)


def _mixin_common(j, x_ref, g_ref, w_ref, wf_ref, bf_ref, lf_ref, lfrep_ref, hn_ref, cb_ref,
                  n_heads):
    @pl.when(j == 0)
    def _():
        hn = _rms(x_ref[...], g_ref[...]).astype(BF16)
        hn_ref[...] = hn
        cb_ref[...] = _dot(hn, w_ref[...])
        lf = _log_sigmoid(_dot(hn, wf_ref[...]) + bf_ref[...])
        lf_ref[...] = lf[:, :n_heads]
        if lfrep_ref is not None:
            lfrep_ref[...] = lf

    return lambda: _dot(hn_ref[...], w_ref[...])


def _conv_taps(u, uext_ref, tm):
    uext_ref[pl.ds(SUBLANES, tm), :] = u
    um1 = uext_ref[pl.ds(SUBLANES - 1, tm), :]
    um2 = uext_ref[pl.ds(SUBLANES - 2, tm), :]
    return um1, um2


def _mixin_prompt_kernel(x_ref, g_ref, w_ref, wf_ref, bf_ref, cw_ref, gc_ref,
                         zc_ref, q_ref, k_ref, v_ref, kb_ref, vb_ref, lf_ref, lfrep_ref, cs_ref,
                         hn_ref, cb_ref, cc_ref, uext_ref, *, tm, tiles_per_seq, n_heads, scale):
    i = pl.program_id(0)
    j = pl.program_id(1)
    proj = _mixin_common(j, x_ref, g_ref, w_ref, wf_ref, bf_ref, lf_ref, lfrep_ref, hn_ref, cb_ref,
                         n_heads)

    @pl.when(j == 1)
    def _():
        cc_ref[...] = proj()

    @pl.when(j == 2)
    def _():
        @pl.when(i % tiles_per_seq == 0)
        def _():
            uext_ref[pl.ds(0, SUBLANES), :] = jnp.zeros((SUBLANES, cc_ref.shape[1]), F32)

        u = cc_ref[...] * proj()
        um1, um2 = _conv_taps(u, uext_ref, tm)
        y = cb_ref[...] * (cw_ref[0:1, :] * um2 + cw_ref[1:2, :] * um1 + cw_ref[2:3, :] * u)
        zc_ref[...] = _rms(y, gc_ref[...]).astype(BF16)
        cs_ref[0] = u[tm - (CONV_W - 1):, :]
        uext_ref[pl.ds(0, SUBLANES), :] = u[tm - SUBLANES:, :]

    @pl.when(j == 3)
    def _():
        q_ref[...] = (proj() * scale).astype(BF16)

    @pl.when(j == 4)
    def _():
        z = proj()
        k_ref[...] = z
        kb_ref[...] = z.astype(BF16)

    @pl.when(j == 5)
    def _():
        z = proj()
        v_ref[...] = z
        vb_ref[...] = z.astype(BF16)


def _mixin_prompt(x, g, w_main, w_f, b_f, cw, gc, seq, n_heads, scale):
    t, d = x.shape
    c = cw.shape[1]
    tm = TOKEN_TILE
    row = lambda i, j: (i, 0)
    fixed = lambda i, j: (0, 0)
    kern = functools.partial(_mixin_prompt_kernel, tm=tm, tiles_per_seq=seq // tm,
                             n_heads=n_heads, scale=scale)
    return pl.pallas_call(
        kern,
        grid=(t // tm, 6),
        in_specs=[
            pl.BlockSpec((tm, d), row),
            pl.BlockSpec((1, d), fixed),
            pl.BlockSpec((d, c), lambda i, j: (0, j)),
            pl.BlockSpec((d, LANES), fixed),
            pl.BlockSpec((1, LANES), fixed),
            pl.BlockSpec((CONV_W, c), fixed),
            pl.BlockSpec((1, c), fixed),
        ],
        out_specs=[
            pl.BlockSpec((tm, c), row),
            pl.BlockSpec((tm, c), row),
            pl.BlockSpec((tm, c), row),
            pl.BlockSpec((tm, c), row),
            pl.BlockSpec((tm, c), row),
            pl.BlockSpec((tm, c), row),
            pl.BlockSpec((tm, n_heads), row),
            pl.BlockSpec((tm, LANES), row),
            pl.BlockSpec((1, CONV_W - 1, c), lambda i, j: (i // (seq // tm), 0, 0)),
        ],
        out_shape=[
            jax.ShapeDtypeStruct((t, c), BF16),
            jax.ShapeDtypeStruct((t, c), BF16),
            jax.ShapeDtypeStruct((t, c), F32),
            jax.ShapeDtypeStruct((t, c), F32),
            jax.ShapeDtypeStruct((t, c), BF16),
            jax.ShapeDtypeStruct((t, c), BF16),
            jax.ShapeDtypeStruct((t, n_heads), F32),
            jax.ShapeDtypeStruct((t, LANES), F32),
            jax.ShapeDtypeStruct((t // seq, CONV_W - 1, c), F32),
        ],
        scratch_shapes=[
            pltpu.VMEM((tm, d), BF16),
            pltpu.VMEM((tm, c), F32),
            pltpu.VMEM((tm, c), F32),
            pltpu.VMEM((tm + SUBLANES, c), F32),
        ],
        compiler_params=_params("arbitrary", "arbitrary"),
        name="mixer_in_prompt",
    )(x, g, w_main, w_f, b_f, cw, gc)


def _mixin_sample_kernel(x_ref, g_ref, w_ref, wf_ref, bf_ref, cw_ref, gc_ref, s1_ref, s2_ref,
                         zc_ref, q_ref, k_ref, v_ref, u_ref, lf_ref,
                         hn_ref, cb_ref, cc_ref, uext_ref, *, tm, dec_seq, n_heads, scale):
    j = pl.program_id(1)
    proj = _mixin_common(j, x_ref, g_ref, w_ref, wf_ref, bf_ref, lf_ref, None, hn_ref, cb_ref,
                         n_heads)

    @pl.when(j == 1)
    def _():
        cc_ref[...] = proj()

    @pl.when(j == 2)
    def _():
        uext_ref[pl.ds(0, SUBLANES), :] = jnp.zeros((SUBLANES, cc_ref.shape[1]), F32)
        u = cc_ref[...] * proj()
        um1, um2 = _conv_taps(u, uext_ref, tm)
        step = lax.broadcasted_iota(jnp.int32, u.shape, 0) % dec_seq
        um1 = jnp.where(step >= 1, um1, s1_ref[...])
        um2 = jnp.where(step >= 2, um2, s2_ref[...])
        y = cb_ref[...] * (cw_ref[0:1, :] * um2 + cw_ref[1:2, :] * um1 + cw_ref[2:3, :] * u)
        zc_ref[...] = _rms(y, gc_ref[...]).astype(BF16)
        u_ref[...] = u

    @pl.when(j == 3)
    def _():
        q_ref[...] = proj() * scale

    @pl.when(j == 4)
    def _():
        k_ref[...] = proj()

    @pl.when(j == 5)
    def _():
        v_ref[...] = proj()


def _mixin_sample(x, g, w_main, w_f, b_f, cw, gc, s1, s2, dec_seq, n_heads, scale):
    t, d = x.shape
    c = cw.shape[1]
    tm = t
    row = lambda i, j: (i, 0)
    fixed = lambda i, j: (0, 0)
    kern = functools.partial(_mixin_sample_kernel, tm=tm, dec_seq=dec_seq, n_heads=n_heads,
                             scale=scale)
    return pl.pallas_call(
        kern,
        grid=(1, 6),
        in_specs=[
            pl.BlockSpec((tm, d), row),
            pl.BlockSpec((1, d), fixed),
            pl.BlockSpec((d, c), lambda i, j: (0, j)),
            pl.BlockSpec((d, LANES), fixed),
            pl.BlockSpec((1, LANES), fixed),
            pl.BlockSpec((CONV_W, c), fixed),
            pl.BlockSpec((1, c), fixed),
            pl.BlockSpec((tm, c), row),
            pl.BlockSpec((tm, c), row),
        ],
        out_specs=[
            pl.BlockSpec((tm, c), row),
            pl.BlockSpec((tm, c), row),
            pl.BlockSpec((tm, c), row),
            pl.BlockSpec((tm, c), row),
            pl.BlockSpec((tm, c), row),
            pl.BlockSpec((tm, n_heads), row),
        ],
        out_shape=[
            jax.ShapeDtypeStruct((t, c), BF16),
            jax.ShapeDtypeStruct((t, c), F32),
            jax.ShapeDtypeStruct((t, c), F32),
            jax.ShapeDtypeStruct((t, c), F32),
            jax.ShapeDtypeStruct((t, c), F32),
            jax.ShapeDtypeStruct((t, n_heads), F32),
        ],
        scratch_shapes=[
            pltpu.VMEM((tm, d), BF16),
            pltpu.VMEM((tm, c), F32),
            pltpu.VMEM((tm, c), F32),
            pltpu.VMEM((tm + SUBLANES, c), F32),
        ],
        compiler_params=_params("arbitrary", "arbitrary"),
        name="mixer_in_sample",
    )(x, g, w_main, w_f, b_f, cw, gc, s1, s2)


def _split3(x):
    hi = x.astype(BF16)
    r1 = x - hi.astype(F32)
    mid = r1.astype(BF16)
    lo = (r1 - mid.astype(F32)).astype(BF16)
    return hi, mid, lo


def _attn_prompt_kernel(q_ref, k_ref, v_ref, lfrep_ref, ga_ref, za_ref, aug_ref, o_ref,
                        *, blk, n_heads):
    i = pl.program_id(1)
    seq = k_ref.shape[0]
    lane = lax.broadcasted_iota(jnp.int32, (blk, LANES), 1)
    row = lax.broadcasted_iota(jnp.int32, (blk, blk), 0)
    col = lax.broadcasted_iota(jnp.int32, (blk, blk), 1)
    causal = col <= row

    @pl.when(i == 0)
    def _():
        tri = jnp.where(causal, 1.0, 0.0).astype(BF16)
        carry = jnp.zeros((1, LANES), F32)
        for jb in range(seq // blk):
            rows = slice(jb * blk, (jb + 1) * blk)
            hi, mid, lo = _split3(lfrep_ref[rows, :])
            f = _dot(tri, hi) + _dot(tri, mid) + _dot(tri, lo) + carry
            carry = f[blk - 1:blk, :]
            hi, mid, lo = _split3(f * (-LOG2E))
            zero = jnp.zeros_like(hi)
            aug_ref[rows, :] = jnp.where(
                lane < n_heads, hi,
                jnp.where(lane < 2 * n_heads, mid, jnp.where(lane < 3 * n_heads, lo, zero)))

    def head_out(h, nblk):
        hs = slice(h * HEAD_DIM, (h + 1) * HEAD_DIM)
        ones = jnp.where((lane % n_heads == h) & (lane < 3 * n_heads), 1.0, 0.0).astype(BF16)
        qa = jnp.concatenate([q_ref[:, hs], ones], axis=1)
        scores = []
        for jb in range(nblk):
            rows = slice(jb * blk, (jb + 1) * blk)
            sj = _dot_nt(qa, jnp.concatenate([k_ref[rows, hs], aug_ref[rows, :]], axis=1))
            scores.append(jnp.where(causal, sj, MASK_VALUE) if jb == nblk - 1 else sj)
        mx = scores[0]
        for sj in scores[1:]:
            mx = jnp.maximum(mx, sj)
        m = jnp.max(mx, axis=-1, keepdims=True)
        acc = psum = None
        for jb, sj in enumerate(scores):
            p = jnp.exp2(sj - m)
            o = _dot(p.astype(BF16), v_ref[jb * blk:(jb + 1) * blk, hs])
            psum = p if psum is None else psum + p
            acc = o if acc is None else acc + o
        return acc / jnp.sum(psum, axis=-1, keepdims=True)

    for nblk in range(1, seq // blk + 1):
        @pl.when(i == nblk - 1)
        def _():
            for h in range(n_heads):
                o_ref[:, h * HEAD_DIM:(h + 1) * HEAD_DIM] = head_out(h, nblk)

    za_ref[...] = _rms(o_ref[...], ga_ref[...]).astype(BF16)


def _attn_prompt(q, kb, vb, lfrep, ga, seq, n_heads):
    t, c = q.shape
    b = t // seq
    blk = ATTN_BLOCK
    nq = seq // blk
    kern = functools.partial(_attn_prompt_kernel, blk=blk, n_heads=n_heads)
    return pl.pallas_call(
        kern,
        grid=(b, nq),
        in_specs=[
            pl.BlockSpec((blk, c), lambda bi, i: (bi * nq + i, 0)),
            pl.BlockSpec((seq, c), lambda bi, i: (bi, 0)),
            pl.BlockSpec((seq, c), lambda bi, i: (bi, 0)),
            pl.BlockSpec((seq, LANES), lambda bi, i: (bi, 0)),
            pl.BlockSpec((1, c), lambda bi, i: (0, 0)),
        ],
        out_specs=pl.BlockSpec((blk, c), lambda bi, i: (bi * nq + i, 0)),
        out_shape=jax.ShapeDtypeStruct((t, c), BF16),
        scratch_shapes=[pltpu.VMEM((seq, LANES), BF16), pltpu.VMEM((blk, c), F32)],
        compiler_params=_params("arbitrary", "arbitrary"),
        name="fox_prompt",
    )(q, kb, vb, lfrep, ga)


def _split3_dot(x, w):
    hi, mid, lo = _split3(x)
    return _dot(hi, w) + _dot(mid, w) + _dot(lo, w)


def _select_sum_kernel(x_ref, w_ref, o_ref):
    o_ref[...] = _split3_dot(x_ref[...], w_ref[...])


def _select_sum(x, w, tm):
    rows, k = x.shape
    n = w.shape[1]
    return pl.pallas_call(
        _select_sum_kernel,
        grid=(rows // tm,),
        in_specs=[pl.BlockSpec((tm, k), lambda i: (i, 0)), pl.BlockSpec((k, n), lambda i: (0, 0))],
        out_specs=pl.BlockSpec((tm, n), lambda i: (i, 0)),
        out_shape=jax.ShapeDtypeStruct((rows, n), F32),
        compiler_params=_params("parallel"),
        name="head_major_prefix_sum",
    )(x, w)


def _prefix_matrix(n_steps, n_heads, width):
    r = jnp.arange(n_steps * n_heads)
    cidx = jnp.arange(n_heads * width)
    t, h = r // n_heads, r % n_heads
    h2, t2 = cidx // width, cidx % width
    w = (h[:, None] == h2[None, :]) & (t[:, None] <= t2[None, :]) & (t2[None, :] < n_steps)
    return w.astype(BF16)


def _attn_sample_kernel(pt_ref, q_ref, kn_ref, vn_ref, fn_ref, ga_ref, ck_hbm, cv_hbm, pf_hbm,
                        za_ref, kbuf_ref, vbuf_ref, pfbuf_ref, sem, qbd_ref, kb_ref, vb_ref,
                        m_ref, l_ref, acc_ref, run_ref, *, pages, groups, n_steps, layer, n_heads,
                        dec_seq):
    g = pl.program_id(1)
    step = pl.program_id(0) * groups + g
    slot = step % 2

    def page_copies(step_idx, slot_idx):
        bb = step_idx // groups
        first = (step_idx % groups) * pages
        out = []
        for r in range(pages):
            page = pt_ref[bb, first + r]
            out.append(pltpu.make_async_copy(ck_hbm.at[layer, page], kbuf_ref.at[slot_idx, r],
                                             sem.at[slot_idx, 0]))
            out.append(pltpu.make_async_copy(cv_hbm.at[layer, page], vbuf_ref.at[slot_idx, r],
                                             sem.at[slot_idx, 1]))
            out.append(pltpu.make_async_copy(pf_hbm.at[page], pfbuf_ref.at[slot_idx, r],
                                             sem.at[slot_idx, 2]))
        return out

    @pl.when(step == 0)
    def _():
        for cp in page_copies(0, 0):
            cp.start()

    for cp in page_copies(jnp.minimum(step + 1, n_steps - 1), 1 - slot):
        cp.start()

    for cp in page_copies(step, slot):
        cp.wait()

    rows = dec_seq * n_heads
    c = n_heads * HEAD_DIM
    head_of_row = lax.broadcasted_iota(jnp.int32, (n_heads, c), 0)
    head_of_col = lax.broadcasted_iota(jnp.int32, (n_heads, c), 1) // HEAD_DIM
    diag = head_of_row == head_of_col

    @pl.when(g == 0)
    def _():
        for t in range(dec_seq):
            qt = jnp.broadcast_to(q_ref[0, t:t + 1, :], (n_heads, c))
            qbd_ref[t * n_heads:(t + 1) * n_heads, :] = jnp.where(diag, qt, 0.0).astype(BF16)
        m_ref[...] = jnp.full(m_ref.shape, MASK_VALUE, F32)
        l_ref[...] = jnp.zeros(l_ref.shape, F32)
        acc_ref[...] = jnp.zeros(acc_ref.shape, F32)
        run_ref[...] = jnp.zeros(run_ref.shape, F32)

    def online_update(s, v_bf):
        m_old = m_ref[...]
        m_new = jnp.maximum(m_old, jnp.max(s, axis=-1, keepdims=True))
        alpha = jnp.exp(m_old - m_new)
        p = jnp.exp(s - m_new)
        l_ref[...] = alpha * l_ref[...] + jnp.sum(p, axis=-1, keepdims=True)
        acc_ref[...] = alpha * acc_ref[...] + _dot(p.astype(BF16), v_bf)
        m_ref[...] = m_new

    run = run_ref[...]
    bias_parts = []
    for pg in range(pages):
        toks = slice(pg * PAGE_SIZE, (pg + 1) * PAGE_SIZE)
        for h in range(n_heads):
            hs = slice(h * HEAD_DIM, (h + 1) * HEAD_DIM)
            head_rows = pl.ds(h, PAGE_SIZE, stride=n_heads)
            kb_ref[toks, hs] = kbuf_ref[slot, pg, head_rows, :].astype(BF16)
            vb_ref[toks, hs] = vbuf_ref[slot, pg, head_rows, :].astype(BF16)
        pf = pfbuf_ref[slot, pg]
        bias_parts.append(run + pf)
        run = run + jnp.broadcast_to(pf[:, PAGE_SIZE - 1:PAGE_SIZE], pf.shape)
    run_ref[...] = run
    bias = jnp.concatenate(bias_parts, axis=1)
    bias = jnp.concatenate([bias] * dec_seq, axis=0)
    s = _dot_nt(qbd_ref[...], kb_ref[...]) - bias
    online_update(s, vb_ref[...])

    @pl.when(g == pl.num_programs(1) - 1)
    def _():
        pad = jnp.zeros((PAGE_SIZE - SUBLANES, c), F32)
        kn = jnp.concatenate([kn_ref[0], pad], axis=0).astype(BF16)
        vn = jnp.concatenate([vn_ref[0], pad], axis=0).astype(BF16)
        bias_n = jnp.concatenate([run + fn_ref[0]] * dec_seq, axis=0)
        s_n = _dot_nt(qbd_ref[...], kn) - bias_n
        q_step = lax.broadcasted_iota(jnp.int32, (rows, PAGE_SIZE), 0) // n_heads
        k_step = lax.broadcasted_iota(jnp.int32, (rows, PAGE_SIZE), 1)
        s_n = jnp.where(k_step <= q_step, s_n, MASK_VALUE)
        online_update(s_n, vn)
        o = acc_ref[...] / l_ref[...]
        for t in range(dec_seq):
            ot = jnp.where(diag, o[t * n_heads:(t + 1) * n_heads, :], 0.0)
            ot = jnp.sum(ot, axis=0, keepdims=True)
            za_ref[0, t:t + 1, :] = _rms(ot, ga_ref[...])

    @pl.when(step == n_steps - 1)
    def _():
        for cp in page_copies(n_steps - 1, 1 - slot):
            cp.wait()


def _attn_sample(page_table, q, k_new, v_new, f_new, ga, cache_k, cache_v, pfx, layer, n_heads,
                 dec_seq):
    nb, n_pages = page_table.shape
    c = n_heads * HEAD_DIM
    pages = PAGES_PER_STEP
    groups = n_pages // pages
    rows = dec_seq * n_heads
    per_b = lambda b, g, pt: (b, 0, 0)
    kern = functools.partial(_attn_sample_kernel, pages=pages, groups=groups, n_steps=nb * groups,
                             layer=layer, n_heads=n_heads, dec_seq=dec_seq)
    grid_spec = pltpu.PrefetchScalarGridSpec(
        num_scalar_prefetch=1,
        grid=(nb, groups),
        in_specs=[
            pl.BlockSpec((1, dec_seq, c), per_b),
            pl.BlockSpec((1, SUBLANES, c), per_b),
            pl.BlockSpec((1, SUBLANES, c), per_b),
            pl.BlockSpec((1, n_heads, LANES), per_b),
            pl.BlockSpec((1, c), lambda b, g, pt: (0, 0)),
            pl.BlockSpec(memory_space=pl.ANY),
            pl.BlockSpec(memory_space=pl.ANY),
            pl.BlockSpec(memory_space=pl.ANY),
        ],
        out_specs=pl.BlockSpec((1, dec_seq, c), per_b),
        scratch_shapes=[
            pltpu.VMEM((2, pages, PAGE_SIZE * n_heads, HEAD_DIM), F32),
            pltpu.VMEM((2, pages, PAGE_SIZE * n_heads, HEAD_DIM), F32),
            pltpu.VMEM((2, pages, n_heads, LANES), F32),
            pltpu.SemaphoreType.DMA((2, 3)),
            pltpu.VMEM((rows, c), BF16),
            pltpu.VMEM((pages * PAGE_SIZE, c), BF16),
            pltpu.VMEM((pages * PAGE_SIZE, c), BF16),
            pltpu.VMEM((rows, 1), F32),
            pltpu.VMEM((rows, 1), F32),
            pltpu.VMEM((rows, c), F32),
            pltpu.VMEM((n_heads, LANES), F32),
        ],
    )
    return pl.pallas_call(
        kern,
        grid_spec=grid_spec,
        out_shape=jax.ShapeDtypeStruct((nb, dec_seq, c), F32),
        compiler_params=_params("arbitrary", "arbitrary"),
        name="fox_sample",
    )(page_table, q, k_new, v_new, f_new, ga, cache_k, cache_v, pfx)


def _resident(block_shape, index_map):
    return pl.BlockSpec(block_shape, index_map, pipeline_mode=pl.Buffered(1))


def _mixout_kernel(h_ref, zc_ref, za_ref, wc_ref, wa_ref, o_ref, wcb_ref, wab_ref):
    @pl.when(pl.program_id(0) == 0)
    def _():
        wcb_ref[...] = wc_ref[...].astype(BF16)
        wab_ref[...] = wa_ref[...].astype(BF16)

    o_ref[...] = (h_ref[...] + _dot(zc_ref[...].astype(BF16), wcb_ref[...])
                  + _dot(za_ref[...].astype(BF16), wab_ref[...]))


def _mixout(h, zc, za, w_out, layer, tm):
    t, d = h.shape
    c = zc.shape[1]
    assert za.shape[1] == c and w_out.shape[1] == 2 * c
    row = lambda i: (i, 0)
    return pl.pallas_call(
        _mixout_kernel,
        grid=(t // tm,),
        in_specs=[pl.BlockSpec((tm, d), row), pl.BlockSpec((tm, c), row), pl.BlockSpec((tm, c), row),
                  _resident((None, c, d), lambda i: (layer, 0, 0)),
                  _resident((None, c, d), lambda i: (layer, 1, 0))],
        out_specs=pl.BlockSpec((tm, d), row),
        out_shape=jax.ShapeDtypeStruct((t, d), F32),
        scratch_shapes=[pltpu.VMEM((c, d), BF16), pltpu.VMEM((c, d), BF16)],
        compiler_params=_params("arbitrary"),
        name="mixer_out",
    )(h, zc, za, w_out, w_out)


def _ple_kernel(h_ref, p_ref, g_ref, wg_ref, wp_ref, gf_ref, o_ref, wgb_ref, wpb_ref, *, final):
    @pl.when(pl.program_id(0) == 0)
    def _():
        wgb_ref[...] = wg_ref[...].astype(BF16)
        wpb_ref[...] = wp_ref[...].astype(BF16)

    h = h_ref[...]
    hn = _rms(h, g_ref[...]).astype(BF16)
    gate = jax.nn.sigmoid(_dot(hn, wgb_ref[...]))
    out = h + gate * _dot(p_ref[...].astype(BF16), wpb_ref[...])
    o_ref[...] = _rms(out, gf_ref[...]) if final else out


def _ple(h, p, g, wg, wp, gf, layer, final, tm):
    t, d = h.shape
    pd = p.shape[1]
    row = lambda i: (i, 0)
    fixed = lambda i: (0, 0)
    return pl.pallas_call(
        functools.partial(_ple_kernel, final=final),
        grid=(t // tm,),
        in_specs=[pl.BlockSpec((tm, d), row), pl.BlockSpec((tm, pd), row), pl.BlockSpec((1, d), fixed),
                  _resident((None, d, d), lambda i: (layer, 0, 0)),
                  _resident((None, pd, d), lambda i: (layer, 0, 0)),
                  pl.BlockSpec((1, d), fixed)],
        out_specs=pl.BlockSpec((tm, d), row),
        out_shape=jax.ShapeDtypeStruct((t, d), F32),
        scratch_shapes=[pltpu.VMEM((d, d), BF16), pltpu.VMEM((pd, d), BF16)],
        compiler_params=_params("arbitrary"),
        name="ple",
    )(h, p, g, wg, wp, gf)


def kernel(x_prompt, x_sample, p_prompt, p_sample, cache_k, cache_v, cache_logf, state_conv, page_table, norm_ffn1, w_ffn1_gate, w_ffn1_up, w_ffn1_down, norm_mix, w_in, b_f, conv_w, norm_conv_out, norm_attn_out, w_out, norm_ffn2, w_ffn2_gate, w_ffn2_up, w_ffn2_down, norm_ple, w_ple_gate, w_ple_proj, norm_final):
    batch, seq, d = x_prompt.shape
    dec_batch, dec_seq, _ = x_sample.shape
    depth = w_in.shape[0]
    n_heads = b_f.shape[1]
    conv_dim = conv_w.shape[2]
    attn_dim = n_heads * HEAD_DIM
    n_pool = cache_k.shape[1]
    scale = HEAD_DIM ** -0.5
    n_main = 3 * conv_dim + 3 * attn_dim
    assert conv_dim == attn_dim and dec_seq >= CONV_W - 1 and dec_seq <= SUBLANES
    tp, ts = batch * seq, dec_batch * dec_seq

    hp = x_prompt.reshape(tp, d)
    hs = x_sample.reshape(ts, d)
    row2 = lambda a: a.reshape(1, -1)
    page_prefix_w = _prefix_matrix(PAGE_SIZE, n_heads, PAGE_SIZE)
    new_prefix_w = _prefix_matrix(dec_seq, n_heads, LANES)

    outs = [[] for _ in range(8)]
    for l in range(depth):
        w_main = w_in[l].astype(BF16)
        w_f = jnp.tile(w_in[l, :, n_main:], (1, LANES // n_heads)).astype(BF16)
        g1, gm, g2, gp = row2(norm_ffn1[l]), row2(norm_mix[l]), row2(norm_ffn2[l]), row2(norm_ple[l])
        gc, ga = row2(norm_conv_out[l]), row2(norm_attn_out[l])
        bf = row2(jnp.tile(b_f[l], LANES // n_heads))
        cw = conv_w[l]
        last = l == depth - 1
        gfin = row2(norm_final)

        hp, hs = _ffn(hp, hs, g1, w_ffn1_gate, w_ffn1_up, w_ffn1_down, l)

        zc, q, k, v, kb, vb, lf, lfrep, cs = _mixin_prompt(hp, gm, w_main, w_f, bf, cw, gc, seq,
                                                           n_heads, scale * LOG2E)
        za = _attn_prompt(q, kb, vb, lfrep, ga, seq, n_heads)
        hp = _mixout(hp, zc, za, w_out, l, TOKEN_TILE)
        outs[0].append(k.reshape(batch, seq, n_heads, HEAD_DIM))
        outs[1].append(v.reshape(batch, seq, n_heads, HEAD_DIM))
        outs[2].append(lf.reshape(batch, seq, n_heads))
        outs[3].append(cs)

        st = state_conv[l]
        zero = jnp.zeros((dec_batch, dec_seq - 1, conv_dim), F32)
        s1 = jnp.concatenate([st[:, 1:2], zero], axis=1).reshape(ts, conv_dim)
        s2 = jnp.concatenate([st, zero[:, 1:]], axis=1).reshape(ts, conv_dim)
        zc, q, k, v, u, lf = _mixin_sample(hs, gm, w_main, w_f, bf, cw, gc, s1, s2,
                                           dec_seq, n_heads, scale)
        pfx = _select_sum(cache_logf[l].reshape(n_pool, PAGE_SIZE * n_heads), page_prefix_w, 512)
        pfx = pfx.reshape(n_pool, n_heads, PAGE_SIZE)
        f_new = _select_sum(lf.reshape(dec_batch, dec_seq * n_heads), new_prefix_w, dec_batch)
        f_new = f_new.reshape(dec_batch, n_heads, LANES)
        pad8 = lambda a: jnp.pad(a.reshape(dec_batch, dec_seq, attn_dim),
                                 ((0, 0), (0, SUBLANES - dec_seq), (0, 0)))
        za = _attn_sample(page_table, q.reshape(dec_batch, dec_seq, attn_dim), pad8(k), pad8(v),
                          f_new, ga,
                          cache_k.reshape(depth, n_pool, PAGE_SIZE * n_heads, HEAD_DIM),
                          cache_v.reshape(depth, n_pool, PAGE_SIZE * n_heads, HEAD_DIM),
                          pfx, l, n_heads, dec_seq)
        hs = _mixout(hs, zc, za.reshape(ts, attn_dim), w_out, l, ts)

        hp, hs = _ffn(hp, hs, g2, w_ffn2_gate, w_ffn2_up, w_ffn2_down, l)
        hp = _ple(hp, p_prompt[l].reshape(tp, -1), gp, w_ple_gate, w_ple_proj, gfin, l, last,
                  TOKEN_TILE)
        hs = _ple(hs, p_sample[l].reshape(ts, -1), gp, w_ple_gate, w_ple_proj, gfin, l, last, ts)
        outs[4].append(k.reshape(dec_batch, dec_seq, n_heads, HEAD_DIM))
        outs[5].append(v.reshape(dec_batch, dec_seq, n_heads, HEAD_DIM))
        outs[6].append(lf.reshape(dec_batch, dec_seq, n_heads))
        outs[7].append(u.reshape(dec_batch, dec_seq, conv_dim)[:, dec_seq - (CONV_W - 1):])

    return (hp.reshape(batch, seq, d), hs.reshape(dec_batch, dec_seq, d),
            *(jnp.stack(o) for o in outs))
```

```python
import functools

import jax
import jax.numpy as jnp
from jax import lax
from jax.experimental import pallas as pl
from jax.experimental.pallas import tpu as pltpu

F32 = jnp.float32
BF16 = jnp.bfloat16

EPS = 1e-6
HEAD_DIM = 128
PAGE_SIZE = 128
CONV_W = 3
LANES = 128
SUBLANES = 8
MASK_VALUE = -1e30
VMEM_LIMIT = 56 * 1024 * 1024

TOKEN_TILE = 512
MIXIN_TOKEN_TILE = 256
FFN_TOKEN_TILE = 1024
FFN_FF_TILE = 256
ATTN_BLOCK = 256
LOG2E = 1.4426950408889634
PAGES_PER_STEP = 16


def _params(*sem):
    return pltpu.CompilerParams(dimension_semantics=sem, vmem_limit_bytes=VMEM_LIMIT)


def _rms(x, g):
    ms = jnp.mean(x * x, axis=-1, keepdims=True)
    return x * lax.rsqrt(ms + EPS) * g


def _dot(a, b):
    return jnp.dot(a, b, preferred_element_type=F32)


def _dot_nt(a, b):
    return lax.dot_general(a, b, (((1,), (1,)), ((), ())), preferred_element_type=F32)


def _log_sigmoid(x):
    return jnp.minimum(x, 0.0) - jnp.log1p(jnp.exp(-jnp.abs(x)))


def _ffn_kernel(x_hbm, xs_ref, g_ref, wg_ref, wu_ref, wd_ref, o_ref, os_ref,
                xbuf_ref, hn_ref, hns_ref, sem, *, tm):
    i = pl.program_id(0)
    j = pl.program_id(1)
    last_tile = pl.num_programs(0) - 1

    def x_copy(tile):
        return pltpu.make_async_copy(x_hbm.at[pl.ds(tile * tm, tm), :], xbuf_ref, sem)

    @pl.when((i == 0) & (j == 0))
    def _():
        x_copy(0).start()

    @pl.when(j == 0)
    def _():
        x_copy(i).wait()
        x = xbuf_ref[...]
        hn_ref[...] = _rms(x, g_ref[...]).astype(BF16)
        o_ref[...] = x

    @pl.when((j == 1) & (i < last_tile))
    def _():
        x_copy(i + 1).start()

    wg = wg_ref[...].astype(BF16)
    wu = wu_ref[...].astype(BF16)
    wd = wd_ref[...].astype(BF16)

    def half_step(hn):
        gate = _dot(hn, wg)
        up = _dot(hn, wu)
        act = (gate * jax.nn.sigmoid(gate) * (0.5 * up)).astype(BF16)
        return _dot(act, wd)

    o_ref[...] += half_step(hn_ref[...])

    @pl.when(i == last_tile)
    def _():
        @pl.when(j == 0)
        def _():
            xs = xs_ref[...]
            hns_ref[...] = _rms(xs, g_ref[...]).astype(BF16)
            os_ref[...] = xs

        os_ref[...] += half_step(hns_ref[...])


def _ffn(x, xs, g, wg, wu, wd, layer):
    t, d = x.shape
    ts = xs.shape[0]
    f = wg.shape[2]
    tm, tf = FFN_TOKEN_TILE, FFN_FF_TILE
    assert f % tf == 0 and f // tf >= 2 and t % tm == 0
    return pl.pallas_call(
        functools.partial(_ffn_kernel, tm=tm),
        grid=(t // tm, f // tf),
        in_specs=[
            pl.BlockSpec(memory_space=pl.ANY),
            pl.BlockSpec((ts, d), lambda i, j: (0, 0)),
            pl.BlockSpec((1, d), lambda i, j: (0, 0)),
            pl.BlockSpec((None, d, tf), lambda i, j: (layer, 0, j)),
            pl.BlockSpec((None, d, tf), lambda i, j: (layer, 0, j)),
            pl.BlockSpec((None, tf, d), lambda i, j: (layer, j, 0)),
        ],
        out_specs=[pl.BlockSpec((tm, d), lambda i, j: (i, 0)),
                   pl.BlockSpec((ts, d), lambda i, j: (0, 0))],
        out_shape=[jax.ShapeDtypeStruct((t, d), F32), jax.ShapeDtypeStruct((ts, d), F32)],
        scratch_shapes=[pltpu.VMEM((tm, d), F32), pltpu.VMEM((tm, d), BF16),
                        pltpu.VMEM((ts, d), BF16), pltpu.SemaphoreType.DMA(())],
        compiler_params=_params("arbitrary", "arbitrary"),
        name="ffn_half",
    )(x, xs, g, wg, wu, wd)


def _mixin_common(j, x_ref, g_ref, w_ref, wf_ref, bf_ref, lf_ref, lfrep_ref, hn_ref, cb_ref,
                  n_heads):
    @pl.when(j == 0)
    def _():
        hn = _rms(x_ref[...], g_ref[...]).astype(BF16)
        hn_ref[...] = hn
        cb_ref[...] = _dot(hn, w_ref[...])
        lf = _log_sigmoid(_dot(hn, wf_ref[...]) + bf_ref[...])
        lf_ref[...] = lf[:, :n_heads]
        if lfrep_ref is not None:
            lfrep_ref[...] = lf

    return lambda: _dot(hn_ref[...], w_ref[...])


def _conv_taps(u, uext_ref, tm):
    uext_ref[pl.ds(SUBLANES, tm), :] = u
    um1 = uext_ref[pl.ds(SUBLANES - 1, tm), :]
    um2 = uext_ref[pl.ds(SUBLANES - 2, tm), :]
    return um1, um2


def _mixin_prompt_kernel(x_ref, g_ref, w_ref, wf_ref, bf_ref, cw_ref, gc_ref,
                         zc_ref, q_ref, k_ref, v_ref, kb_ref, vb_ref, lf_ref, lfrep_ref, cs_ref,
                         uext_ref, *, tm, tiles_per_seq, n_heads, scale):
    i = pl.program_id(0)
    c = cw_ref.shape[1]
    hn = _rms(x_ref[...], g_ref[...]).astype(BF16)
    proj = lambda grp: _dot(hn, w_ref[:, grp * c:(grp + 1) * c])

    lf = _log_sigmoid(_dot(hn, wf_ref[...]) + bf_ref[...])
    lf_ref[...] = lf[:, :n_heads]
    lfrep_ref[...] = lf

    cb = proj(0)
    u = proj(1) * proj(2)
    head = jnp.where(i % tiles_per_seq == 0, 0.0, uext_ref[pl.ds(0, SUBLANES), :])
    uext_ref[pl.ds(0, SUBLANES), :] = head
    um1, um2 = _conv_taps(u, uext_ref, tm)
    y = cb * (cw_ref[0:1, :] * um2 + cw_ref[1:2, :] * um1 + cw_ref[2:3, :] * u)
    zc_ref[...] = _rms(y, gc_ref[...]).astype(BF16)
    cs_ref[0] = u[tm - (CONV_W - 1):, :]
    uext_ref[pl.ds(0, SUBLANES), :] = u[tm - SUBLANES:, :]

    q_ref[...] = (proj(3) * scale).astype(BF16)
    z = proj(4)
    k_ref[...] = z
    kb_ref[...] = z.astype(BF16)
    z = proj(5)
    v_ref[...] = z
    vb_ref[...] = z.astype(BF16)


def _mixin_prompt(x, g, w_main, w_f, b_f, cw, gc, seq, n_heads, scale):
    t, d = x.shape
    c = cw.shape[1]
    tm = MIXIN_TOKEN_TILE
    row = lambda i: (i, 0)
    fixed = lambda i: (0, 0)
    kern = functools.partial(_mixin_prompt_kernel, tm=tm, tiles_per_seq=seq // tm,
                             n_heads=n_heads, scale=scale)
    return pl.pallas_call(
        kern,
        grid=(t // tm,),
        in_specs=[
            pl.BlockSpec((tm, d), row),
            pl.BlockSpec((1, d), fixed),
            pl.BlockSpec(w_main.shape, fixed, pipeline_mode=pl.Buffered(1)),
            pl.BlockSpec((d, LANES), fixed),
            pl.BlockSpec((1, LANES), fixed),
            pl.BlockSpec((CONV_W, c), fixed),
            pl.BlockSpec((1, c), fixed),
        ],
        out_specs=[
            pl.BlockSpec((tm, c), row),
            pl.BlockSpec((tm, c), row),
            pl.BlockSpec((tm, c), row),
            pl.BlockSpec((tm, c), row),
            pl.BlockSpec((tm, c), row),
            pl.BlockSpec((tm, c), row),
            pl.BlockSpec((tm, n_heads), row),
            pl.BlockSpec((tm, LANES), row),
            pl.BlockSpec((1, CONV_W - 1, c), lambda i: (i // (seq // tm), 0, 0)),
        ],
        out_shape=[
            jax.ShapeDtypeStruct((t, c), BF16),
            jax.ShapeDtypeStruct((t, c), BF16),
            jax.ShapeDtypeStruct((t, c), F32),
            jax.ShapeDtypeStruct((t, c), F32),
            jax.ShapeDtypeStruct((t, c), BF16),
            jax.ShapeDtypeStruct((t, c), BF16),
            jax.ShapeDtypeStruct((t, n_heads), F32),
            jax.ShapeDtypeStruct((t, LANES), F32),
            jax.ShapeDtypeStruct((t // seq, CONV_W - 1, c), F32),
        ],
        scratch_shapes=[pltpu.VMEM((tm + SUBLANES, c), F32)],
        compiler_params=_params("arbitrary"),
        name="mixer_in_prompt",
    )(x, g, w_main, w_f, b_f, cw, gc)


def _mixin_sample_kernel(x_ref, g_ref, w_ref, wf_ref, bf_ref, cw_ref, gc_ref, s1_ref, s2_ref,
                         zc_ref, q_ref, k_ref, v_ref, u_ref, lf_ref,
                         hn_ref, cb_ref, cc_ref, uext_ref, *, tm, dec_seq, n_heads, scale):
    j = pl.program_id(1)
    proj = _mixin_common(j, x_ref, g_ref, w_ref, wf_ref, bf_ref, lf_ref, None, hn_ref, cb_ref,
                         n_heads)

    @pl.when(j == 1)
    def _():
        cc_ref[...] = proj()

    @pl.when(j == 2)
    def _():
        uext_ref[pl.ds(0, SUBLANES), :] = jnp.zeros((SUBLANES, cc_ref.shape[1]), F32)
        u = cc_ref[...] * proj()
        um1, um2 = _conv_taps(u, uext_ref, tm)
        step = lax.broadcasted_iota(jnp.int32, u.shape, 0) % dec_seq
        um1 = jnp.where(step >= 1, um1, s1_ref[...])
        um2 = jnp.where(step >= 2, um2, s2_ref[...])
        y = cb_ref[...] * (cw_ref[0:1, :] * um2 + cw_ref[1:2, :] * um1 + cw_ref[2:3, :] * u)
        zc_ref[...] = _rms(y, gc_ref[...]).astype(BF16)
        u_ref[...] = u

    @pl.when(j == 3)
    def _():
        q_ref[...] = proj() * scale

    @pl.when(j == 4)
    def _():
        k_ref[...] = proj()

    @pl.when(j == 5)
    def _():
        v_ref[...] = proj()


def _mixin_sample(x, g, w_main, w_f, b_f, cw, gc, s1, s2, dec_seq, n_heads, scale):
    t, d = x.shape
    c = cw.shape[1]
    tm = t
    row = lambda i, j: (i, 0)
    fixed = lambda i, j: (0, 0)
    kern = functools.partial(_mixin_sample_kernel, tm=tm, dec_seq=dec_seq, n_heads=n_heads,
                             scale=scale)
    return pl.pallas_call(
        kern,
        grid=(1, 6),
        in_specs=[
            pl.BlockSpec((tm, d), row),
            pl.BlockSpec((1, d), fixed),
            pl.BlockSpec((d, c), lambda i, j: (0, j)),
            pl.BlockSpec((d, LANES), fixed),
            pl.BlockSpec((1, LANES), fixed),
            pl.BlockSpec((CONV_W, c), fixed),
            pl.BlockSpec((1, c), fixed),
            pl.BlockSpec((tm, c), row),
            pl.BlockSpec((tm, c), row),
        ],
        out_specs=[
            pl.BlockSpec((tm, c), row),
            pl.BlockSpec((tm, c), row),
            pl.BlockSpec((tm, c), row),
            pl.BlockSpec((tm, c), row),
            pl.BlockSpec((tm, c), row),
            pl.BlockSpec((tm, n_heads), row),
        ],
        out_shape=[
            jax.ShapeDtypeStruct((t, c), BF16),
            jax.ShapeDtypeStruct((t, c), F32),
            jax.ShapeDtypeStruct((t, c), F32),
            jax.ShapeDtypeStruct((t, c), F32),
            jax.ShapeDtypeStruct((t, c), F32),
            jax.ShapeDtypeStruct((t, n_heads), F32),
        ],
        scratch_shapes=[
            pltpu.VMEM((tm, d), BF16),
            pltpu.VMEM((tm, c), F32),
            pltpu.VMEM((tm, c), F32),
            pltpu.VMEM((tm + SUBLANES, c), F32),
        ],
        compiler_params=_params("arbitrary", "arbitrary"),
        name="mixer_in_sample",
    )(x, g, w_main, w_f, b_f, cw, gc, s1, s2)


def _split3(x):
    hi = x.astype(BF16)
    r1 = x - hi.astype(F32)
    mid = r1.astype(BF16)
    lo = (r1 - mid.astype(F32)).astype(BF16)
    return hi, mid, lo


def _attn_prompt_kernel(q_ref, k_ref, v_ref, lfrep_ref, ga_ref, za_ref, aug_ref, o_ref,
                        *, blk, n_heads):
    i = pl.program_id(1)
    seq = k_ref.shape[0]
    lane = lax.broadcasted_iota(jnp.int32, (blk, LANES), 1)
    row = lax.broadcasted_iota(jnp.int32, (blk, blk), 0)
    col = lax.broadcasted_iota(jnp.int32, (blk, blk), 1)
    causal = col <= row

    @pl.when(i == 0)
    def _():
        tri = jnp.where(causal, 1.0, 0.0).astype(BF16)
        carry = jnp.zeros((1, LANES), F32)
        for jb in range(seq // blk):
            rows = slice(jb * blk, (jb + 1) * blk)
            hi, mid, lo = _split3(lfrep_ref[rows, :])
            f = _dot(tri, hi) + _dot(tri, mid) + _dot(tri, lo) + carry
            carry = f[blk - 1:blk, :]
            hi, mid, lo = _split3(f * (-LOG2E))
            zero = jnp.zeros_like(hi)
            aug_ref[rows, :] = jnp.where(
                lane < n_heads, hi,
                jnp.where(lane < 2 * n_heads, mid, jnp.where(lane < 3 * n_heads, lo, zero)))

    def head_out(h, nblk):
        hs = slice(h * HEAD_DIM, (h + 1) * HEAD_DIM)
        ones = jnp.where((lane % n_heads == h) & (lane < 3 * n_heads), 1.0, 0.0).astype(BF16)
        qa = jnp.concatenate([q_ref[:, hs], ones], axis=1)
        scores = []
        for jb in range(nblk):
            rows = slice(jb * blk, (jb + 1) * blk)
            sj = _dot_nt(qa, jnp.concatenate([k_ref[rows, hs], aug_ref[rows, :]], axis=1))
            scores.append(jnp.where(causal, sj, MASK_VALUE) if jb == nblk - 1 else sj)
        mx = scores[0]
        for sj in scores[1:]:
            mx = jnp.maximum(mx, sj)
        m = jnp.max(mx, axis=-1, keepdims=True)
        acc = psum = None
        for jb, sj in enumerate(scores):
            p = jnp.exp2(sj - m)
            o = _dot(p.astype(BF16), v_ref[jb * blk:(jb + 1) * blk, hs])
            psum = p if psum is None else psum + p
            acc = o if acc is None else acc + o
        return acc / jnp.sum(psum, axis=-1, keepdims=True)

    for nblk in range(1, seq // blk + 1):
        @pl.when(i == nblk - 1)
        def _():
            for h in range(n_heads):
                o_ref[:, h * HEAD_DIM:(h + 1) * HEAD_DIM] = head_out(h, nblk)

    za_ref[...] = _rms(o_ref[...], ga_ref[...]).astype(BF16)


def _attn_prompt(q, kb, vb, lfrep, ga, seq, n_heads):
    t, c = q.shape
    b = t // seq
    blk = ATTN_BLOCK
    nq = seq // blk
    kern = functools.partial(_attn_prompt_kernel, blk=blk, n_heads=n_heads)
    return pl.pallas_call(
        kern,
        grid=(b, nq),
        in_specs=[
            pl.BlockSpec((blk, c), lambda bi, i: (bi * nq + i, 0)),
            pl.BlockSpec((seq, c), lambda bi, i: (bi, 0)),
            pl.BlockSpec((seq, c), lambda bi, i: (bi, 0)),
            pl.BlockSpec((seq, LANES), lambda bi, i: (bi, 0)),
            pl.BlockSpec((1, c), lambda bi, i: (0, 0)),
        ],
        out_specs=pl.BlockSpec((blk, c), lambda bi, i: (bi * nq + i, 0)),
        out_shape=jax.ShapeDtypeStruct((t, c), BF16),
        scratch_shapes=[pltpu.VMEM((seq, LANES), BF16), pltpu.VMEM((blk, c), F32)],
        compiler_params=_params("arbitrary", "arbitrary"),
        name="fox_prompt",
    )(q, kb, vb, lfrep, ga)


def _split3_dot(x, w):
    hi, mid, lo = _split3(x)
    return _dot(hi, w) + _dot(mid, w) + _dot(lo, w)


def _select_sum_kernel(x_ref, w_ref, o_ref):
    o_ref[...] = _split3_dot(x_ref[...], w_ref[...])


def _select_sum(x, w, tm):
    rows, k = x.shape
    n = w.shape[1]
    return pl.pallas_call(
        _select_sum_kernel,
        grid=(rows // tm,),
        in_specs=[pl.BlockSpec((tm, k), lambda i: (i, 0)), pl.BlockSpec((k, n), lambda i: (0, 0))],
        out_specs=pl.BlockSpec((tm, n), lambda i: (i, 0)),
        out_shape=jax.ShapeDtypeStruct((rows, n), F32),
        compiler_params=_params("parallel"),
        name="head_major_prefix_sum",
    )(x, w)


def _prefix_matrix(n_steps, n_heads, width):
    r = jnp.arange(n_steps * n_heads)
    cidx = jnp.arange(n_heads * width)
    t, h = r // n_heads, r % n_heads
    h2, t2 = cidx // width, cidx % width
    w = (h[:, None] == h2[None, :]) & (t[:, None] <= t2[None, :]) & (t2[None, :] < n_steps)
    return w.astype(BF16)


def _attn_sample_kernel(pt_ref, q_ref, kn_ref, vn_ref, fn_ref, ga_ref, ck_hbm, cv_hbm, pf_hbm,
                        za_ref, kbuf_ref, vbuf_ref, pfbuf_ref, sem, qbd_ref, kb_ref, vb_ref,
                        m_ref, l_ref, acc_ref, run_ref, *, pages, groups, n_steps, layer, n_heads,
                        dec_seq):
    g = pl.program_id(1)
    step = pl.program_id(0) * groups + g
    slot = step % 2

    def page_copies(step_idx, slot_idx):
        bb = step_idx // groups
        first = (step_idx % groups) * pages
        out = []
        for r in range(pages):
            page = pt_ref[bb, first + r]
            out.append(pltpu.make_async_copy(ck_hbm.at[layer, page], kbuf_ref.at[slot_idx, r],
                                             sem.at[slot_idx, 0]))
            out.append(pltpu.make_async_copy(cv_hbm.at[layer, page], vbuf_ref.at[slot_idx, r],
                                             sem.at[slot_idx, 1]))
            out.append(pltpu.make_async_copy(pf_hbm.at[page], pfbuf_ref.at[slot_idx, r],
                                             sem.at[slot_idx, 2]))
        return out

    @pl.when(step == 0)
    def _():
        for cp in page_copies(0, 0):
            cp.start()

    for cp in page_copies(jnp.minimum(step + 1, n_steps - 1), 1 - slot):
        cp.start()

    for cp in page_copies(step, slot):
        cp.wait()

    rows = dec_seq * n_heads
    c = n_heads * HEAD_DIM
    head_of_row = lax.broadcasted_iota(jnp.int32, (n_heads, c), 0)
    head_of_col = lax.broadcasted_iota(jnp.int32, (n_heads, c), 1) // HEAD_DIM
    diag = head_of_row == head_of_col

    @pl.when(g == 0)
    def _():
        for t in range(dec_seq):
            qt = jnp.broadcast_to(q_ref[0, t:t + 1, :], (n_heads, c))
            qbd_ref[t * n_heads:(t + 1) * n_heads, :] = jnp.where(diag, qt, 0.0).astype(BF16)
        m_ref[...] = jnp.full(m_ref.shape, MASK_VALUE, F32)
        l_ref[...] = jnp.zeros(l_ref.shape, F32)
        acc_ref[...] = jnp.zeros(acc_ref.shape, F32)
        run_ref[...] = jnp.zeros(run_ref.shape, F32)

    def online_update(s, v_bf):
        m_old = m_ref[...]
        m_new = jnp.maximum(m_old, jnp.max(s, axis=-1, keepdims=True))
        alpha = jnp.exp(m_old - m_new)
        p = jnp.exp(s - m_new)
        l_ref[...] = alpha * l_ref[...] + jnp.sum(p, axis=-1, keepdims=True)
        acc_ref[...] = alpha * acc_ref[...] + _dot(p.astype(BF16), v_bf)
        m_ref[...] = m_new

    run = run_ref[...]
    bias_parts = []
    for pg in range(pages):
        toks = slice(pg * PAGE_SIZE, (pg + 1) * PAGE_SIZE)
        for h in range(n_heads):
            hs = slice(h * HEAD_DIM, (h + 1) * HEAD_DIM)
            head_rows = pl.ds(h, PAGE_SIZE, stride=n_heads)
            kb_ref[toks, hs] = kbuf_ref[slot, pg, head_rows, :].astype(BF16)
            vb_ref[toks, hs] = vbuf_ref[slot, pg, head_rows, :].astype(BF16)
        pf = pfbuf_ref[slot, pg]
        bias_parts.append(run + pf)
        run = run + jnp.broadcast_to(pf[:, PAGE_SIZE - 1:PAGE_SIZE], pf.shape)
    run_ref[...] = run
    bias = jnp.concatenate(bias_parts, axis=1)
    bias = jnp.concatenate([bias] * dec_seq, axis=0)
    s = _dot_nt(qbd_ref[...], kb_ref[...]) - bias
    online_update(s, vb_ref[...])

    @pl.when(g == pl.num_programs(1) - 1)
    def _():
        pad = jnp.zeros((PAGE_SIZE - SUBLANES, c), F32)
        kn = jnp.concatenate([kn_ref[0], pad], axis=0).astype(BF16)
        vn = jnp.concatenate([vn_ref[0], pad], axis=0).astype(BF16)
        bias_n = jnp.concatenate([run + fn_ref[0]] * dec_seq, axis=0)
        s_n = _dot_nt(qbd_ref[...], kn) - bias_n
        q_step = lax.broadcasted_iota(jnp.int32, (rows, PAGE_SIZE), 0) // n_heads
        k_step = lax.broadcasted_iota(jnp.int32, (rows, PAGE_SIZE), 1)
        s_n = jnp.where(k_step <= q_step, s_n, MASK_VALUE)
        online_update(s_n, vn)
        o = acc_ref[...] / l_ref[...]
        for t in range(dec_seq):
            ot = jnp.where(diag, o[t * n_heads:(t + 1) * n_heads, :], 0.0)
            ot = jnp.sum(ot, axis=0, keepdims=True)
            za_ref[0, t:t + 1, :] = _rms(ot, ga_ref[...])

    @pl.when(step == n_steps - 1)
    def _():
        for cp in page_copies(n_steps - 1, 1 - slot):
            cp.wait()


def _attn_sample(page_table, q, k_new, v_new, f_new, ga, cache_k, cache_v, pfx, layer, n_heads,
                 dec_seq):
    nb, n_pages = page_table.shape
    c = n_heads * HEAD_DIM
    pages = PAGES_PER_STEP
    groups = n_pages // pages
    rows = dec_seq * n_heads
    per_b = lambda b, g, pt: (b, 0, 0)
    kern = functools.partial(_attn_sample_kernel, pages=pages, groups=groups, n_steps=nb * groups,
                             layer=layer, n_heads=n_heads, dec_seq=dec_seq)
    grid_spec = pltpu.PrefetchScalarGridSpec(
        num_scalar_prefetch=1,
        grid=(nb, groups),
        in_specs=[
            pl.BlockSpec((1, dec_seq, c), per_b),
            pl.BlockSpec((1, SUBLANES, c), per_b),
            pl.BlockSpec((1, SUBLANES, c), per_b),
            pl.BlockSpec((1, n_heads, LANES), per_b),
            pl.BlockSpec((1, c), lambda b, g, pt: (0, 0)),
            pl.BlockSpec(memory_space=pl.ANY),
            pl.BlockSpec(memory_space=pl.ANY),
            pl.BlockSpec(memory_space=pl.ANY),
        ],
        out_specs=pl.BlockSpec((1, dec_seq, c), per_b),
        scratch_shapes=[
            pltpu.VMEM((2, pages, PAGE_SIZE * n_heads, HEAD_DIM), F32),
            pltpu.VMEM((2, pages, PAGE_SIZE * n_heads, HEAD_DIM), F32),
            pltpu.VMEM((2, pages, n_heads, LANES), F32),
            pltpu.SemaphoreType.DMA((2, 3)),
            pltpu.VMEM((rows, c), BF16),
            pltpu.VMEM((pages * PAGE_SIZE, c), BF16),
            pltpu.VMEM((pages * PAGE_SIZE, c), BF16),
            pltpu.VMEM((rows, 1), F32),
            pltpu.VMEM((rows, 1), F32),
            pltpu.VMEM((rows, c), F32),
            pltpu.VMEM((n_heads, LANES), F32),
        ],
    )
    return pl.pallas_call(
        kern,
        grid_spec=grid_spec,
        out_shape=jax.ShapeDtypeStruct((nb, dec_seq, c), F32),
        compiler_params=_params("arbitrary", "arbitrary"),
        name="fox_sample",
    )(page_table, q, k_new, v_new, f_new, ga, cache_k, cache_v, pfx)


def _resident(block_shape, index_map):
    return pl.BlockSpec(block_shape, index_map, pipeline_mode=pl.Buffered(1))


def _mixout_kernel(h_ref, zc_ref, za_ref, wc_ref, wa_ref, o_ref, wcb_ref, wab_ref):
    @pl.when(pl.program_id(0) == 0)
    def _():
        wcb_ref[...] = wc_ref[...].astype(BF16)
        wab_ref[...] = wa_ref[...].astype(BF16)

    o_ref[...] = (h_ref[...] + _dot(zc_ref[...].astype(BF16), wcb_ref[...])
                  + _dot(za_ref[...].astype(BF16), wab_ref[...]))


def _mixout(h, zc, za, w_out, layer, tm):
    t, d = h.shape
    c = zc.shape[1]
    assert za.shape[1] == c and w_out.shape[1] == 2 * c
    row = lambda i: (i, 0)
    return pl.pallas_call(
        _mixout_kernel,
        grid=(t // tm,),
        in_specs=[pl.BlockSpec((tm, d), row), pl.BlockSpec((tm, c), row), pl.BlockSpec((tm, c), row),
                  _resident((None, c, d), lambda i: (layer, 0, 0)),
                  _resident((None, c, d), lambda i: (layer, 1, 0))],
        out_specs=pl.BlockSpec((tm, d), row),
        out_shape=jax.ShapeDtypeStruct((t, d), F32),
        scratch_shapes=[pltpu.VMEM((c, d), BF16), pltpu.VMEM((c, d), BF16)],
        compiler_params=_params("arbitrary"),
        name="mixer_out",
    )(h, zc, za, w_out, w_out)


def _ple_kernel(h_ref, p_ref, g_ref, wg_ref, wp_ref, gf_ref, o_ref, wgb_ref, wpb_ref, *, final):
    @pl.when(pl.program_id(0) == 0)
    def _():
        wgb_ref[...] = wg_ref[...].astype(BF16)
        wpb_ref[...] = wp_ref[...].astype(BF16)

    h = h_ref[...]
    hn = _rms(h, g_ref[...]).astype(BF16)
    gate = jax.nn.sigmoid(_dot(hn, wgb_ref[...]))
    out = h + gate * _dot(p_ref[...].astype(BF16), wpb_ref[...])
    o_ref[...] = _rms(out, gf_ref[...]) if final else out


def _ple(h, p, g, wg, wp, gf, layer, final, tm):
    t, d = h.shape
    pd = p.shape[1]
    row = lambda i: (i, 0)
    fixed = lambda i: (0, 0)
    return pl.pallas_call(
        functools.partial(_ple_kernel, final=final),
        grid=(t // tm,),
        in_specs=[pl.BlockSpec((tm, d), row), pl.BlockSpec((tm, pd), row), pl.BlockSpec((1, d), fixed),
                  _resident((None, d, d), lambda i: (layer, 0, 0)),
                  _resident((None, pd, d), lambda i: (layer, 0, 0)),
                  pl.BlockSpec((1, d), fixed)],
        out_specs=pl.BlockSpec((tm, d), row),
        out_shape=jax.ShapeDtypeStruct((t, d), F32),
        scratch_shapes=[pltpu.VMEM((d, d), BF16), pltpu.VMEM((pd, d), BF16)],
        compiler_params=_params("arbitrary"),
        name="ple",
    )(h, p, g, wg, wp, gf)


def kernel(x_prompt, x_sample, p_prompt, p_sample, cache_k, cache_v, cache_logf, state_conv, page_table, norm_ffn1, w_ffn1_gate, w_ffn1_up, w_ffn1_down, norm_mix, w_in, b_f, conv_w, norm_conv_out, norm_attn_out, w_out, norm_ffn2, w_ffn2_gate, w_ffn2_up, w_ffn2_down, norm_ple, w_ple_gate, w_ple_proj, norm_final):
    batch, seq, d = x_prompt.shape
    dec_batch, dec_seq, _ = x_sample.shape
    depth = w_in.shape[0]
    n_heads = b_f.shape[1]
    conv_dim = conv_w.shape[2]
    attn_dim = n_heads * HEAD_DIM
    n_pool = cache_k.shape[1]
    scale = HEAD_DIM ** -0.5
    n_main = 3 * conv_dim + 3 * attn_dim
    assert conv_dim == attn_dim and dec_seq >= CONV_W - 1 and dec_seq <= SUBLANES
    tp, ts = batch * seq, dec_batch * dec_seq

    hp = x_prompt.reshape(tp, d)
    hs = x_sample.reshape(ts, d)
    row2 = lambda a: a.reshape(1, -1)
    page_prefix_w = _prefix_matrix(PAGE_SIZE, n_heads, PAGE_SIZE)
    new_prefix_w = _prefix_matrix(dec_seq, n_heads, LANES)

    outs = [[] for _ in range(8)]
    for l in range(depth):
        w_main = w_in[l].astype(BF16)
        w_f = jnp.tile(w_in[l, :, n_main:], (1, LANES // n_heads)).astype(BF16)
        g1, gm, g2, gp = row2(norm_ffn1[l]), row2(norm_mix[l]), row2(norm_ffn2[l]), row2(norm_ple[l])
        gc, ga = row2(norm_conv_out[l]), row2(norm_attn_out[l])
        bf = row2(jnp.tile(b_f[l], LANES // n_heads))
        cw = conv_w[l]
        last = l == depth - 1
        gfin = row2(norm_final)

        hp, hs = _ffn(hp, hs, g1, w_ffn1_gate, w_ffn1_up, w_ffn1_down, l)

        zc, q, k, v, kb, vb, lf, lfrep, cs = _mixin_prompt(hp, gm, w_main, w_f, bf, cw, gc, seq,
                                                           n_heads, scale * LOG2E)
        za = _attn_prompt(q, kb, vb, lfrep, ga, seq, n_heads)
        hp = _mixout(hp, zc, za, w_out, l, TOKEN_TILE)
        outs[0].append(k.reshape(batch, seq, n_heads, HEAD_DIM))
        outs[1].append(v.reshape(batch, seq, n_heads, HEAD_DIM))
        outs[2].append(lf.reshape(batch, seq, n_heads))
        outs[3].append(cs)

        st = state_conv[l]
        zero = jnp.zeros((dec_batch, dec_seq - 1, conv_dim), F32)
        s1 = jnp.concatenate([st[:, 1:2], zero], axis=1).reshape(ts, conv_dim)
        s2 = jnp.concatenate([st, zero[:, 1:]], axis=1).reshape(ts, conv_dim)
        zc, q, k, v, u, lf = _mixin_sample(hs, gm, w_main, w_f, bf, cw, gc, s1, s2,
                                           dec_seq, n_heads, scale)
        pfx = _select_sum(cache_logf[l].reshape(n_pool, PAGE_SIZE * n_heads), page_prefix_w, 512)
        pfx = pfx.reshape(n_pool, n_heads, PAGE_SIZE)
        f_new = _select_sum(lf.reshape(dec_batch, dec_seq * n_heads), new_prefix_w, dec_batch)
        f_new = f_new.reshape(dec_batch, n_heads, LANES)
        pad8 = lambda a: jnp.pad(a.reshape(dec_batch, dec_seq, attn_dim),
                                 ((0, 0), (0, SUBLANES - dec_seq), (0, 0)))
        za = _attn_sample(page_table, q.reshape(dec_batch, dec_seq, attn_dim), pad8(k), pad8(v),
                          f_new, ga,
                          cache_k.reshape(depth, n_pool, PAGE_SIZE * n_heads, HEAD_DIM),
                          cache_v.reshape(depth, n_pool, PAGE_SIZE * n_heads, HEAD_DIM),
                          pfx, l, n_heads, dec_seq)
        hs = _mixout(hs, zc, za.reshape(ts, attn_dim), w_out, l, ts)

        hp, hs = _ffn(hp, hs, g2, w_ffn2_gate, w_ffn2_up, w_ffn2_down, l)
        hp = _ple(hp, p_prompt[l].reshape(tp, -1), gp, w_ple_gate, w_ple_proj, gfin, l, last,
                  TOKEN_TILE)
        hs = _ple(hs, p_sample[l].reshape(ts, -1), gp, w_ple_gate, w_ple_proj, gfin, l, last, ts)
        outs[4].append(k.reshape(dec_batch, dec_seq, n_heads, HEAD_DIM))
        outs[5].append(v.reshape(dec_batch, dec_seq, n_heads, HEAD_DIM))
        outs[6].append(lf.reshape(dec_batch, dec_seq, n_heads))
        outs[7].append(u.reshape(dec_batch, dec_seq, conv_dim)[:, dec_seq - (CONV_W - 1):])

    return (hp.reshape(batch, seq, d), hs.reshape(dec_batch, dec_seq, d),
            *(jnp.stack(o) for o in outs))
```

```python
import functools

import jax
import jax.numpy as jnp
from jax import lax
from jax.experimental import pallas as pl
from jax.experimental.pallas import tpu as pltpu

F32 = jnp.float32
BF16 = jnp.bfloat16

EPS = 1e-6
HEAD_DIM = 128
PAGE_SIZE = 128
CONV_W = 3
LANES = 128
SUBLANES = 8
MASK_VALUE = -1e30
VMEM_LIMIT = 56 * 1024 * 1024

TOKEN_TILE = 512
MIXIN_TOKEN_TILE = 256
FFN_TOKEN_TILE = 1024
FFN_FF_TILE = 256
ATTN_BLOCK = 256
LOG2E = 1.4426950408889634
PAGES_PER_STEP = 16


def _params(*sem):
    return pltpu.CompilerParams(dimension_semantics=sem, vmem_limit_bytes=VMEM_LIMIT)


def _rms(x, g):
    ms = jnp.mean(x * x, axis=-1, keepdims=True)
    return x * lax.rsqrt(ms + EPS) * g


def _dot(a, b):
    return jnp.dot(a, b, preferred_element_type=F32)


def _dot_nt(a, b):
    return lax.dot_general(a, b, (((1,), (1,)), ((), ())), preferred_element_type=F32)


def _log_sigmoid(x):
    return jnp.minimum(x, 0.0) - jnp.log1p(jnp.exp(-jnp.abs(x)))


def _ffn_kernel(x_hbm, xs_ref, g_ref, wg_ref, wu_ref, wd_ref, o_ref, os_ref,
                xbuf_ref, hn_ref, hns_ref, sem, *, tm):
    i = pl.program_id(0)
    j = pl.program_id(1)
    last_tile = pl.num_programs(0) - 1

    def x_copy(tile):
        return pltpu.make_async_copy(x_hbm.at[pl.ds(tile * tm, tm), :], xbuf_ref, sem)

    @pl.when((i == 0) & (j == 0))
    def _():
        x_copy(0).start()

    @pl.when(j == 0)
    def _():
        x_copy(i).wait()
        x = xbuf_ref[...]
        hn_ref[...] = _rms(x, g_ref[...]).astype(BF16)
        o_ref[...] = x

    @pl.when((j == 1) & (i < last_tile))
    def _():
        x_copy(i + 1).start()

    wg = wg_ref[...].astype(BF16)
    wu = wu_ref[...].astype(BF16)
    wd = wd_ref[...].astype(BF16)

    def half_step(hn):
        gate = _dot(hn, wg)
        up = _dot(hn, wu)
        act = (gate * jax.nn.sigmoid(gate) * (0.5 * up)).astype(BF16)
        return _dot(act, wd)

    o_ref[...] += half_step(hn_ref[...])

    @pl.when(i == last_tile)
    def _():
        @pl.when(j == 0)
        def _():
            xs = xs_ref[...]
            hns_ref[...] = _rms(xs, g_ref[...]).astype(BF16)
            os_ref[...] = xs

        os_ref[...] += half_step(hns_ref[...])


def _ffn(x, xs, g, wg, wu, wd, layer):
    t, d = x.shape
    ts = xs.shape[0]
    f = wg.shape[2]
    tm, tf = FFN_TOKEN_TILE, FFN_FF_TILE
    assert f % tf == 0 and f // tf >= 2 and t % tm == 0
    return pl.pallas_call(
        functools.partial(_ffn_kernel, tm=tm),
        grid=(t // tm, f // tf),
        in_specs=[
            pl.BlockSpec(memory_space=pl.ANY),
            pl.BlockSpec((ts, d), lambda i, j: (0, 0)),
            pl.BlockSpec((1, d), lambda i, j: (0, 0)),
            pl.BlockSpec((None, d, tf), lambda i, j: (layer, 0, j)),
            pl.BlockSpec((None, d, tf), lambda i, j: (layer, 0, j)),
            pl.BlockSpec((None, tf, d), lambda i, j: (layer, j, 0)),
        ],
        out_specs=[pl.BlockSpec((tm, d), lambda i, j: (i, 0)),
                   pl.BlockSpec((ts, d), lambda i, j: (0, 0))],
        out_shape=[jax.ShapeDtypeStruct((t, d), F32), jax.ShapeDtypeStruct((ts, d), F32)],
        scratch_shapes=[pltpu.VMEM((tm, d), F32), pltpu.VMEM((tm, d), BF16),
                        pltpu.VMEM((ts, d), BF16), pltpu.SemaphoreType.DMA(())],
        compiler_params=_params("arbitrary", "arbitrary"),
        name="ffn_half",
    )(x, xs, g, wg, wu, wd)


def _mixin_common(j, x_ref, g_ref, w_ref, wf_ref, bf_ref, lf_ref, lfrep_ref, hn_ref, cb_ref,
                  n_heads):
    @pl.when(j == 0)
    def _():
        hn = _rms(x_ref[...], g_ref[...]).astype(BF16)
        hn_ref[...] = hn
        cb_ref[...] = _dot(hn, w_ref[...])
        lf = _log_sigmoid(_dot(hn, wf_ref[...]) + bf_ref[...])
        lf_ref[...] = lf[:, :n_heads]
        if lfrep_ref is not None:
            lfrep_ref[...] = lf

    return lambda: _dot(hn_ref[...], w_ref[...])


def _conv_taps(u, uext_ref, tm):
    uext_ref[pl.ds(SUBLANES, tm), :] = u
    um1 = uext_ref[pl.ds(SUBLANES - 1, tm), :]
    um2 = uext_ref[pl.ds(SUBLANES - 2, tm), :]
    return um1, um2


def _mixin_prompt_kernel(x_ref, g_ref, w_ref, wf_ref, bf_ref, cw_ref, gc_ref,
                         zc_ref, q_ref, k_ref, v_ref, kb_ref, vb_ref, lf_ref, lfrep_ref, cs_ref,
                         uext_ref, *, tm, tiles_per_seq, n_heads, scale):
    i = pl.program_id(0)
    c = cw_ref.shape[1]
    hn = _rms(x_ref[...], g_ref[...]).astype(BF16)
    proj = lambda grp: _dot(hn, w_ref[:, grp * c:(grp + 1) * c])

    lf = _log_sigmoid(_dot(hn, wf_ref[...]) + bf_ref[...])
    lf_ref[...] = lf[:, :n_heads]
    lfrep_ref[...] = lf

    cb = proj(0)
    u = proj(1) * proj(2)
    head = jnp.where(i % tiles_per_seq == 0, 0.0, uext_ref[pl.ds(0, SUBLANES), :])
    uext_ref[pl.ds(0, SUBLANES), :] = head
    um1, um2 = _conv_taps(u, uext_ref, tm)
    y = cb * (cw_ref[0:1, :] * um2 + cw_ref[1:2, :] * um1 + cw_ref[2:3, :] * u)
    zc_ref[...] = _rms(y, gc_ref[...]).astype(BF16)
    cs_ref[0] = u[tm - (CONV_W - 1):, :]
    uext_ref[pl.ds(0, SUBLANES), :] = u[tm - SUBLANES:, :]

    q_ref[...] = (proj(3) * scale).astype(BF16)
    z = proj(4)
    k_ref[...] = z
    kb_ref[...] = z.astype(BF16)
    z = proj(5)
    v_ref[...] = z
    vb_ref[...] = z.astype(BF16)


def _mixin_prompt(x, g, w_main, w_f, b_f, cw, gc, seq, n_heads, scale):
    t, d = x.shape
    c = cw.shape[1]
    tm = MIXIN_TOKEN_TILE
    row = lambda i: (i, 0)
    fixed = lambda i: (0, 0)
    kern = functools.partial(_mixin_prompt_kernel, tm=tm, tiles_per_seq=seq // tm,
                             n_heads=n_heads, scale=scale)
    return pl.pallas_call(
        kern,
        grid=(t // tm,),
        in_specs=[
            pl.BlockSpec((tm, d), row),
            pl.BlockSpec((1, d), fixed),
            pl.BlockSpec(w_main.shape, fixed, pipeline_mode=pl.Buffered(1)),
            pl.BlockSpec((d, LANES), fixed),
            pl.BlockSpec((1, LANES), fixed),
            pl.BlockSpec((CONV_W, c), fixed),
            pl.BlockSpec((1, c), fixed),
        ],
        out_specs=[
            pl.BlockSpec((tm, c), row),
            pl.BlockSpec((tm, c), row),
            pl.BlockSpec((tm, c), row),
            pl.BlockSpec((tm, c), row),
            pl.BlockSpec((tm, c), row),
            pl.BlockSpec((tm, c), row),
            pl.BlockSpec((tm, n_heads), row),
            pl.BlockSpec((tm, LANES), row),
            pl.BlockSpec((1, CONV_W - 1, c), lambda i: (i // (seq // tm), 0, 0)),
        ],
        out_shape=[
            jax.ShapeDtypeStruct((t, c), BF16),
            jax.ShapeDtypeStruct((t, c), BF16),
            jax.ShapeDtypeStruct((t, c), F32),
            jax.ShapeDtypeStruct((t, c), F32),
            jax.ShapeDtypeStruct((t, c), BF16),
            jax.ShapeDtypeStruct((t, c), BF16),
            jax.ShapeDtypeStruct((t, n_heads), F32),
            jax.ShapeDtypeStruct((t, LANES), F32),
            jax.ShapeDtypeStruct((t // seq, CONV_W - 1, c), F32),
        ],
        scratch_shapes=[pltpu.VMEM((tm + SUBLANES, c), F32)],
        compiler_params=_params("arbitrary"),
        name="mixer_in_prompt",
    )(x, g, w_main, w_f, b_f, cw, gc)


def _mixin_sample_kernel(x_ref, g_ref, w_ref, wf_ref, bf_ref, cw_ref, gc_ref, s1_ref, s2_ref,
                         zc_ref, q_ref, k_ref, v_ref, u_ref, lf_ref,
                         hn_ref, cb_ref, cc_ref, uext_ref, *, tm, dec_seq, n_heads, scale):
    j = pl.program_id(1)
    proj = _mixin_common(j, x_ref, g_ref, w_ref, wf_ref, bf_ref, lf_ref, None, hn_ref, cb_ref,
                         n_heads)

    @pl.when(j == 1)
    def _():
        cc_ref[...] = proj()

    @pl.when(j == 2)
    def _():
        uext_ref[pl.ds(0, SUBLANES), :] = jnp.zeros((SUBLANES, cc_ref.shape[1]), F32)
        u = cc_ref[...] * proj()
        um1, um2 = _conv_taps(u, uext_ref, tm)
        step = lax.broadcasted_iota(jnp.int32, u.shape, 0) % dec_seq
        um1 = jnp.where(step >= 1, um1, s1_ref[...])
        um2 = jnp.where(step >= 2, um2, s2_ref[...])
        y = cb_ref[...] * (cw_ref[0:1, :] * um2 + cw_ref[1:2, :] * um1 + cw_ref[2:3, :] * u)
        zc_ref[...] = _rms(y, gc_ref[...]).astype(BF16)
        u_ref[...] = u

    @pl.when(j == 3)
    def _():
        q_ref[...] = proj() * scale

    @pl.when(j == 4)
    def _():
        k_ref[...] = proj()

    @pl.when(j == 5)
    def _():
        v_ref[...] = proj()


def _mixin_sample(x, g, w_main, w_f, b_f, cw, gc, s1, s2, dec_seq, n_heads, scale):
    t, d = x.shape
    c = cw.shape[1]
    tm = t
    row = lambda i, j: (i, 0)
    fixed = lambda i, j: (0, 0)
    kern = functools.partial(_mixin_sample_kernel, tm=tm, dec_seq=dec_seq, n_heads=n_heads,
                             scale=scale)
    return pl.pallas_call(
        kern,
        grid=(1, 6),
        in_specs=[
            pl.BlockSpec((tm, d), row),
            pl.BlockSpec((1, d), fixed),
            pl.BlockSpec((d, c), lambda i, j: (0, j)),
            pl.BlockSpec((d, LANES), fixed),
            pl.BlockSpec((1, LANES), fixed),
            pl.BlockSpec((CONV_W, c), fixed),
            pl.BlockSpec((1, c), fixed),
            pl.BlockSpec((tm, c), row),
            pl.BlockSpec((tm, c), row),
        ],
        out_specs=[
            pl.BlockSpec((tm, c), row),
            pl.BlockSpec((tm, c), row),
            pl.BlockSpec((tm, c), row),
            pl.BlockSpec((tm, c), row),
            pl.BlockSpec((tm, c), row),
            pl.BlockSpec((tm, n_heads), row),
        ],
        out_shape=[
            jax.ShapeDtypeStruct((t, c), BF16),
            jax.ShapeDtypeStruct((t, c), F32),
            jax.ShapeDtypeStruct((t, c), F32),
            jax.ShapeDtypeStruct((t, c), F32),
            jax.ShapeDtypeStruct((t, c), F32),
            jax.ShapeDtypeStruct((t, n_heads), F32),
        ],
        scratch_shapes=[
            pltpu.VMEM((tm, d), BF16),
            pltpu.VMEM((tm, c), F32),
            pltpu.VMEM((tm, c), F32),
            pltpu.VMEM((tm + SUBLANES, c), F32),
        ],
        compiler_params=_params("arbitrary", "arbitrary"),
        name="mixer_in_sample",
    )(x, g, w_main, w_f, b_f, cw, gc, s1, s2)


def _split3(x):
    hi = x.astype(BF16)
    r1 = x - hi.astype(F32)
    mid = r1.astype(BF16)
    lo = (r1 - mid.astype(F32)).astype(BF16)
    return hi, mid, lo


def _attn_prompt_kernel(q_ref, k_ref, v_ref, lfrep_ref, ga_ref, za_ref, aug_ref, o_ref,
                        *, blk, n_heads):
    i = pl.program_id(1)
    seq = k_ref.shape[0]
    lane = lax.broadcasted_iota(jnp.int32, (blk, LANES), 1)
    row = lax.broadcasted_iota(jnp.int32, (blk, blk), 0)
    col = lax.broadcasted_iota(jnp.int32, (blk, blk), 1)
    causal = col <= row

    @pl.when(i == 0)
    def _():
        tri = jnp.where(causal, 1.0, 0.0).astype(BF16)
        carry = jnp.zeros((1, LANES), F32)
        for jb in range(seq // blk):
            rows = slice(jb * blk, (jb + 1) * blk)
            hi, mid, lo = _split3(lfrep_ref[rows, :])
            f = _dot(tri, hi) + _dot(tri, mid) + _dot(tri, lo) + carry
            carry = f[blk - 1:blk, :]
            hi, mid, lo = _split3(f * (-LOG2E))
            zero = jnp.zeros_like(hi)
            aug_ref[rows, :] = jnp.where(
                lane < n_heads, hi,
                jnp.where(lane < 2 * n_heads, mid, jnp.where(lane < 3 * n_heads, lo, zero)))

    def head_out(h, nblk):
        hs = slice(h * HEAD_DIM, (h + 1) * HEAD_DIM)
        ones = jnp.where((lane % n_heads == h) & (lane < 3 * n_heads), 1.0, 0.0).astype(BF16)
        qa = jnp.concatenate([q_ref[:, hs], ones], axis=1)
        scores = []
        for jb in range(nblk):
            rows = slice(jb * blk, (jb + 1) * blk)
            sj = _dot_nt(qa, jnp.concatenate([k_ref[rows, hs], aug_ref[rows, :]], axis=1))
            scores.append(jnp.where(causal, sj, MASK_VALUE) if jb == nblk - 1 else sj)
        mx = scores[0]
        for sj in scores[1:]:
            mx = jnp.maximum(mx, sj)
        m = jnp.max(mx, axis=-1, keepdims=True)
        acc = psum = None
        for jb, sj in enumerate(scores):
            p = jnp.exp2(sj - m)
            o = _dot(p.astype(BF16), v_ref[jb * blk:(jb + 1) * blk, hs])
            psum = p if psum is None else psum + p
            acc = o if acc is None else acc + o
        return acc / jnp.sum(psum, axis=-1, keepdims=True)

    for nblk in range(1, seq // blk + 1):
        @pl.when(i == nblk - 1)
        def _():
            for h in range(n_heads):
                o_ref[:, h * HEAD_DIM:(h + 1) * HEAD_DIM] = head_out(h, nblk)

    za_ref[...] = _rms(o_ref[...], ga_ref[...]).astype(BF16)


def _attn_prompt(q, kb, vb, lfrep, ga, seq, n_heads):
    t, c = q.shape
    b = t // seq
    blk = ATTN_BLOCK
    nq = seq // blk
    kern = functools.partial(_attn_prompt_kernel, blk=blk, n_heads=n_heads)
    return pl.pallas_call(
        kern,
        grid=(b, nq),
        in_specs=[
            pl.BlockSpec((blk, c), lambda bi, i: (bi * nq + i, 0)),
            pl.BlockSpec((seq, c), lambda bi, i: (bi, 0)),
            pl.BlockSpec((seq, c), lambda bi, i: (bi, 0)),
            pl.BlockSpec((seq, LANES), lambda bi, i: (bi, 0)),
            pl.BlockSpec((1, c), lambda bi, i: (0, 0)),
        ],
        out_specs=pl.BlockSpec((blk, c), lambda bi, i: (bi * nq + i, 0)),
        out_shape=jax.ShapeDtypeStruct((t, c), BF16),
        scratch_shapes=[pltpu.VMEM((seq, LANES), BF16), pltpu.VMEM((blk, c), F32)],
        compiler_params=_params("arbitrary", "arbitrary"),
        name="fox_prompt",
    )(q, kb, vb, lfrep, ga)


def _split3_dot(x, w):
    hi, mid, lo = _split3(x)
    return _dot(hi, w) + _dot(mid, w) + _dot(lo, w)


def _select_sum_kernel(x_ref, w_ref, o_ref):
    o_ref[...] = _split3_dot(x_ref[...], w_ref[...])


def _select_sum(x, w, tm):
    rows, k = x.shape
    n = w.shape[1]
    return pl.pallas_call(
        _select_sum_kernel,
        grid=(rows // tm,),
        in_specs=[pl.BlockSpec((tm, k), lambda i: (i, 0)), pl.BlockSpec((k, n), lambda i: (0, 0))],
        out_specs=pl.BlockSpec((tm, n), lambda i: (i, 0)),
        out_shape=jax.ShapeDtypeStruct((rows, n), F32),
        compiler_params=_params("parallel"),
        name="head_major_prefix_sum",
    )(x, w)


def _prefix_matrix(n_steps, n_heads, width):
    r = jnp.arange(n_steps * n_heads)
    cidx = jnp.arange(n_heads * width)
    t, h = r // n_heads, r % n_heads
    h2, t2 = cidx // width, cidx % width
    w = (h[:, None] == h2[None, :]) & (t[:, None] <= t2[None, :]) & (t2[None, :] < n_steps)
    return w.astype(BF16)


def _attn_sample_kernel(pt_ref, q_ref, kn_ref, vn_ref, fn_ref, ga_ref, ck_hbm, cv_hbm, pf_hbm,
                        za_ref, kbuf_ref, vbuf_ref, pfbuf_ref, sem, qbd_ref, kb_ref, vb_ref,
                        m_ref, l_ref, acc_ref, run_ref, *, pages, groups, n_steps, layer, n_heads,
                        dec_seq):
    g = pl.program_id(1)
    step = pl.program_id(0) * groups + g
    slot = step % 2

    def page_copies(step_idx, slot_idx):
        bb = step_idx // groups
        first = (step_idx % groups) * pages
        out = []
        for r in range(pages):
            page = pt_ref[bb, first + r]
            out.append(pltpu.make_async_copy(ck_hbm.at[layer, page], kbuf_ref.at[slot_idx, r],
                                             sem.at[slot_idx, 0]))
            out.append(pltpu.make_async_copy(cv_hbm.at[layer, page], vbuf_ref.at[slot_idx, r],
                                             sem.at[slot_idx, 1]))
            out.append(pltpu.make_async_copy(pf_hbm.at[page], pfbuf_ref.at[slot_idx, r],
                                             sem.at[slot_idx, 2]))
        return out

    @pl.when(step == 0)
    def _():
        for cp in page_copies(0, 0):
            cp.start()

    for n, cp in enumerate(page_copies(jnp.minimum(step + 1, n_steps - 1), 1 - slot)):
        cp.start(priority=1 if n % 3 == 1 else 0)

    for cp in page_copies(step, slot):
        cp.wait()

    rows = dec_seq * n_heads
    c = n_heads * HEAD_DIM
    head_of_row = lax.broadcasted_iota(jnp.int32, (n_heads, c), 0)
    head_of_col = lax.broadcasted_iota(jnp.int32, (n_heads, c), 1) // HEAD_DIM
    diag = head_of_row == head_of_col

    @pl.when(g == 0)
    def _():
        for t in range(dec_seq):
            qt = jnp.broadcast_to(q_ref[0, t:t + 1, :], (n_heads, c))
            qbd_ref[t * n_heads:(t + 1) * n_heads, :] = jnp.where(diag, qt, 0.0).astype(BF16)
        m_ref[...] = jnp.full(m_ref.shape, MASK_VALUE, F32)
        l_ref[...] = jnp.zeros(l_ref.shape, F32)
        acc_ref[...] = jnp.zeros(acc_ref.shape, F32)
        run_ref[...] = jnp.zeros(run_ref.shape, F32)

    def online_update(s, v_bf):
        m_old = m_ref[...]
        m_new = jnp.maximum(m_old, jnp.max(s, axis=-1, keepdims=True))
        alpha = jnp.exp(m_old - m_new)
        p = jnp.exp(s - m_new)
        l_ref[...] = alpha * l_ref[...] + jnp.sum(p, axis=-1, keepdims=True)
        acc_ref[...] = alpha * acc_ref[...] + _dot(p.astype(BF16), v_bf)
        m_ref[...] = m_new

    run = run_ref[...]
    bias_parts = []
    for pg in range(pages):
        toks = slice(pg * PAGE_SIZE, (pg + 1) * PAGE_SIZE)
        for h in range(n_heads):
            hs = slice(h * HEAD_DIM, (h + 1) * HEAD_DIM)
            head_rows = pl.ds(h, PAGE_SIZE, stride=n_heads)
            kb_ref[toks, hs] = kbuf_ref[slot, pg, head_rows, :].astype(BF16)
            vb_ref[toks, hs] = vbuf_ref[slot, pg, head_rows, :].astype(BF16)
        pf = pfbuf_ref[slot, pg]
        bias_parts.append(run + pf)
        run = run + jnp.broadcast_to(pf[:, PAGE_SIZE - 1:PAGE_SIZE], pf.shape)
    run_ref[...] = run
    bias = jnp.concatenate(bias_parts, axis=1)
    bias = jnp.concatenate([bias] * dec_seq, axis=0)
    s = _dot_nt(qbd_ref[...], kb_ref[...]) - bias
    online_update(s, vb_ref[...])

    @pl.when(g == pl.num_programs(1) - 1)
    def _():
        pad = jnp.zeros((PAGE_SIZE - SUBLANES, c), F32)
        kn = jnp.concatenate([kn_ref[0], pad], axis=0).astype(BF16)
        vn = jnp.concatenate([vn_ref[0], pad], axis=0).astype(BF16)
        bias_n = jnp.concatenate([run + fn_ref[0]] * dec_seq, axis=0)
        s_n = _dot_nt(qbd_ref[...], kn) - bias_n
        q_step = lax.broadcasted_iota(jnp.int32, (rows, PAGE_SIZE), 0) // n_heads
        k_step = lax.broadcasted_iota(jnp.int32, (rows, PAGE_SIZE), 1)
        s_n = jnp.where(k_step <= q_step, s_n, MASK_VALUE)
        online_update(s_n, vn)
        o = acc_ref[...] / l_ref[...]
        for t in range(dec_seq):
            ot = jnp.where(diag, o[t * n_heads:(t + 1) * n_heads, :], 0.0)
            ot = jnp.sum(ot, axis=0, keepdims=True)
            za_ref[0, t:t + 1, :] = _rms(ot, ga_ref[...])

    @pl.when(step == n_steps - 1)
    def _():
        for cp in page_copies(n_steps - 1, 1 - slot):
            cp.wait()


def _attn_sample(page_table, q, k_new, v_new, f_new, ga, cache_k, cache_v, pfx, layer, n_heads,
                 dec_seq):
    nb, n_pages = page_table.shape
    c = n_heads * HEAD_DIM
    pages = PAGES_PER_STEP
    groups = n_pages // pages
    rows = dec_seq * n_heads
    per_b = lambda b, g, pt: (b, 0, 0)
    kern = functools.partial(_attn_sample_kernel, pages=pages, groups=groups, n_steps=nb * groups,
                             layer=layer, n_heads=n_heads, dec_seq=dec_seq)
    grid_spec = pltpu.PrefetchScalarGridSpec(
        num_scalar_prefetch=1,
        grid=(nb, groups),
        in_specs=[
            pl.BlockSpec((1, dec_seq, c), per_b),
            pl.BlockSpec((1, SUBLANES, c), per_b),
            pl.BlockSpec((1, SUBLANES, c), per_b),
            pl.BlockSpec((1, n_heads, LANES), per_b),
            pl.BlockSpec((1, c), lambda b, g, pt: (0, 0)),
            pl.BlockSpec(memory_space=pl.ANY),
            pl.BlockSpec(memory_space=pl.ANY),
            pl.BlockSpec(memory_space=pl.ANY),
        ],
        out_specs=pl.BlockSpec((1, dec_seq, c), per_b),
        scratch_shapes=[
            pltpu.VMEM((2, pages, PAGE_SIZE * n_heads, HEAD_DIM), F32),
            pltpu.VMEM((2, pages, PAGE_SIZE * n_heads, HEAD_DIM), F32),
            pltpu.VMEM((2, pages, n_heads, LANES), F32),
            pltpu.SemaphoreType.DMA((2, 3)),
            pltpu.VMEM((rows, c), BF16),
            pltpu.VMEM((pages * PAGE_SIZE, c), BF16),
            pltpu.VMEM((pages * PAGE_SIZE, c), BF16),
            pltpu.VMEM((rows, 1), F32),
            pltpu.VMEM((rows, 1), F32),
            pltpu.VMEM((rows, c), F32),
            pltpu.VMEM((n_heads, LANES), F32),
        ],
    )
    return pl.pallas_call(
        kern,
        grid_spec=grid_spec,
        out_shape=jax.ShapeDtypeStruct((nb, dec_seq, c), F32),
        compiler_params=_params("arbitrary", "arbitrary"),
        name="fox_sample",
    )(page_table, q, k_new, v_new, f_new, ga, cache_k, cache_v, pfx)


def _resident(block_shape, index_map):
    return pl.BlockSpec(block_shape, index_map, pipeline_mode=pl.Buffered(1))


def _mixout_kernel(h_ref, zc_ref, za_ref, wc_ref, wa_ref, o_ref, wcb_ref, wab_ref):
    @pl.when(pl.program_id(0) == 0)
    def _():
        wcb_ref[...] = wc_ref[...].astype(BF16)
        wab_ref[...] = wa_ref[...].astype(BF16)

    o_ref[...] = (h_ref[...] + _dot(zc_ref[...].astype(BF16), wcb_ref[...])
                  + _dot(za_ref[...].astype(BF16), wab_ref[...]))


def _mixout(h, zc, za, w_out, layer, tm):
    t, d = h.shape
    c = zc.shape[1]
    assert za.shape[1] == c and w_out.shape[1] == 2 * c
    row = lambda i: (i, 0)
    return pl.pallas_call(
        _mixout_kernel,
        grid=(t // tm,),
        in_specs=[pl.BlockSpec((tm, d), row), pl.BlockSpec((tm, c), row), pl.BlockSpec((tm, c), row),
                  _resident((None, c, d), lambda i: (layer, 0, 0)),
                  _resident((None, c, d), lambda i: (layer, 1, 0))],
        out_specs=pl.BlockSpec((tm, d), row),
        out_shape=jax.ShapeDtypeStruct((t, d), F32),
        scratch_shapes=[pltpu.VMEM((c, d), BF16), pltpu.VMEM((c, d), BF16)],
        compiler_params=_params("arbitrary"),
        name="mixer_out",
    )(h, zc, za, w_out, w_out)


def _ple_kernel(h_ref, p_ref, g_ref, wg_ref, wp_ref, gf_ref, o_ref, wgb_ref, wpb_ref, *, final):
    @pl.when(pl.program_id(0) == 0)
    def _():
        wgb_ref[...] = wg_ref[...].astype(BF16)
        wpb_ref[...] = wp_ref[...].astype(BF16)

    h = h_ref[...]
    hn = _rms(h, g_ref[...]).astype(BF16)
    gate = jax.nn.sigmoid(_dot(hn, wgb_ref[...]))
    out = h + gate * _dot(p_ref[...].astype(BF16), wpb_ref[...])
    o_ref[...] = _rms(out, gf_ref[...]) if final else out


def _ple(h, p, g, wg, wp, gf, layer, final, tm):
    t, d = h.shape
    pd = p.shape[1]
    row = lambda i: (i, 0)
    fixed = lambda i: (0, 0)
    return pl.pallas_call(
        functools.partial(_ple_kernel, final=final),
        grid=(t // tm,),
        in_specs=[pl.BlockSpec((tm, d), row), pl.BlockSpec((tm, pd), row), pl.BlockSpec((1, d), fixed),
                  _resident((None, d, d), lambda i: (layer, 0, 0)),
                  _resident((None, pd, d), lambda i: (layer, 0, 0)),
                  pl.BlockSpec((1, d), fixed)],
        out_specs=pl.BlockSpec((tm, d), row),
        out_shape=jax.ShapeDtypeStruct((t, d), F32),
        scratch_shapes=[pltpu.VMEM((d, d), BF16), pltpu.VMEM((pd, d), BF16)],
        compiler_params=_params("arbitrary"),
        name="ple",
    )(h, p, g, wg, wp, gf)


def kernel(x_prompt, x_sample, p_prompt, p_sample, cache_k, cache_v, cache_logf, state_conv, page_table, norm_ffn1, w_ffn1_gate, w_ffn1_up, w_ffn1_down, norm_mix, w_in, b_f, conv_w, norm_conv_out, norm_attn_out, w_out, norm_ffn2, w_ffn2_gate, w_ffn2_up, w_ffn2_down, norm_ple, w_ple_gate, w_ple_proj, norm_final):
    batch, seq, d = x_prompt.shape
    dec_batch, dec_seq, _ = x_sample.shape
    depth = w_in.shape[0]
    n_heads = b_f.shape[1]
    conv_dim = conv_w.shape[2]
    attn_dim = n_heads * HEAD_DIM
    n_pool = cache_k.shape[1]
    scale = HEAD_DIM ** -0.5
    n_main = 3 * conv_dim + 3 * attn_dim
    assert conv_dim == attn_dim and dec_seq >= CONV_W - 1 and dec_seq <= SUBLANES
    tp, ts = batch * seq, dec_batch * dec_seq

    hp = x_prompt.reshape(tp, d)
    hs = x_sample.reshape(ts, d)
    row2 = lambda a: a.reshape(1, -1)
    page_prefix_w = _prefix_matrix(PAGE_SIZE, n_heads, PAGE_SIZE)
    new_prefix_w = _prefix_matrix(dec_seq, n_heads, LANES)

    outs = [[] for _ in range(8)]
    for l in range(depth):
        w_main = w_in[l].astype(BF16)
        w_f = jnp.tile(w_in[l, :, n_main:], (1, LANES // n_heads)).astype(BF16)
        g1, gm, g2, gp = row2(norm_ffn1[l]), row2(norm_mix[l]), row2(norm_ffn2[l]), row2(norm_ple[l])
        gc, ga = row2(norm_conv_out[l]), row2(norm_attn_out[l])
        bf = row2(jnp.tile(b_f[l], LANES // n_heads))
        cw = conv_w[l]
        last = l == depth - 1
        gfin = row2(norm_final)

        hp, hs = _ffn(hp, hs, g1, w_ffn1_gate, w_ffn1_up, w_ffn1_down, l)

        zc, q, k, v, kb, vb, lf, lfrep, cs = _mixin_prompt(hp, gm, w_main, w_f, bf, cw, gc, seq,
                                                           n_heads, scale * LOG2E)
        za = _attn_prompt(q, kb, vb, lfrep, ga, seq, n_heads)
        hp = _mixout(hp, zc, za, w_out, l, TOKEN_TILE)
        outs[0].append(k.reshape(batch, seq, n_heads, HEAD_DIM))
        outs[1].append(v.reshape(batch, seq, n_heads, HEAD_DIM))
        outs[2].append(lf.reshape(batch, seq, n_heads))
        outs[3].append(cs)

        st = state_conv[l]
        zero = jnp.zeros((dec_batch, dec_seq - 1, conv_dim), F32)
        s1 = jnp.concatenate([st[:, 1:2], zero], axis=1).reshape(ts, conv_dim)
        s2 = jnp.concatenate([st, zero[:, 1:]], axis=1).reshape(ts, conv_dim)
        zc, q, k, v, u, lf = _mixin_sample(hs, gm, w_main, w_f, bf, cw, gc, s1, s2,
                                           dec_seq, n_heads, scale)
        pfx = _select_sum(cache_logf[l].reshape(n_pool, PAGE_SIZE * n_heads), page_prefix_w, 512)
        pfx = pfx.reshape(n_pool, n_heads, PAGE_SIZE)
        f_new = _select_sum(lf.reshape(dec_batch, dec_seq * n_heads), new_prefix_w, dec_batch)
        f_new = f_new.reshape(dec_batch, n_heads, LANES)
        pad8 = lambda a: jnp.pad(a.reshape(dec_batch, dec_seq, attn_dim),
                                 ((0, 0), (0, SUBLANES - dec_seq), (0, 0)))
        za = _attn_sample(page_table, q.reshape(dec_batch, dec_seq, attn_dim), pad8(k), pad8(v),
                          f_new, ga,
                          cache_k.reshape(depth, n_pool, PAGE_SIZE * n_heads, HEAD_DIM),
                          cache_v.reshape(depth, n_pool, PAGE_SIZE * n_heads, HEAD_DIM),
                          pfx, l, n_heads, dec_seq)
        hs = _mixout(hs, zc, za.reshape(ts, attn_dim), w_out, l, ts)

        hp, hs = _ffn(hp, hs, g2, w_ffn2_gate, w_ffn2_up, w_ffn2_down, l)
        hp = _ple(hp, p_prompt[l].reshape(tp, -1), gp, w_ple_gate, w_ple_proj, gfin, l, last,
                  TOKEN_TILE)
        hs = _ple(hs, p_sample[l].reshape(ts, -1), gp, w_ple_gate, w_ple_proj, gfin, l, last, ts)
        outs[4].append(k.reshape(dec_batch, dec_seq, n_heads, HEAD_DIM))
        outs[5].append(v.reshape(dec_batch, dec_seq, n_heads, HEAD_DIM))
        outs[6].append(lf.reshape(dec_batch, dec_seq, n_heads))
        outs[7].append(u.reshape(dec_batch, dec_seq, conv_dim)[:, dec_seq - (CONV_W - 1):])

    return (hp.reshape(batch, seq, d), hs.reshape(dec_batch, dec_seq, d),
            *(jnp.stack(o) for o in outs))
```

```python
import functools

import jax
import jax.numpy as jnp
from jax import lax
from jax.experimental import pallas as pl
from jax.experimental.pallas import tpu as pltpu

F32 = jnp.float32
BF16 = jnp.bfloat16

EPS = 1e-6
HEAD_DIM = 128
PAGE_SIZE = 128
CONV_W = 3
LANES = 128
SUBLANES = 8
MASK_VALUE = -1e30
VMEM_LIMIT = 56 * 1024 * 1024

TOKEN_TILE = 512
MIXIN_TOKEN_TILE = 256
FFN_TOKEN_TILE = 1024
FFN_FF_TILE = 256
ATTN_BLOCK = 256
LOG2E = 1.4426950408889634
PAGES_PER_STEP = 16


def _params(*sem):
    return pltpu.CompilerParams(dimension_semantics=sem, vmem_limit_bytes=VMEM_LIMIT)


def _rms(x, g):
    ms = jnp.mean(x * x, axis=-1, keepdims=True)
    return x * lax.rsqrt(ms + EPS) * g


def _dot(a, b):
    return jnp.dot(a, b, preferred_element_type=F32)


def _dot_nt(a, b):
    return lax.dot_general(a, b, (((1,), (1,)), ((), ())), preferred_element_type=F32)


def _log_sigmoid(x):
    return jnp.minimum(x, 0.0) - jnp.log1p(jnp.exp(-jnp.abs(x)))


def _ffn_kernel(x_hbm, xs_ref, g_ref, wg_hbm, wu_hbm, wd_ref, o_ref, os_ref,
                xbuf_ref, hn_ref, hns_ref, wgbuf_ref, wubuf_ref, sem, wsem, *, tm, tf, layer):
    i = pl.program_id(0)
    j = pl.program_id(1)
    n_j = pl.num_programs(1)
    last_tile = pl.num_programs(0) - 1
    step = i * n_j + j
    last_step = pl.num_programs(0) * n_j - 1
    slot = step % 2

    def x_copy(tile):
        return pltpu.make_async_copy(x_hbm.at[pl.ds(tile * tm, tm), :], xbuf_ref, sem)

    def w_copies(step_idx, slot_idx):
        cols = pl.ds((step_idx % n_j) * tf, tf)
        return (pltpu.make_async_copy(wg_hbm.at[layer, :, cols], wgbuf_ref.at[slot_idx],
                                      wsem.at[slot_idx, 0]),
                pltpu.make_async_copy(wu_hbm.at[layer, :, cols], wubuf_ref.at[slot_idx],
                                      wsem.at[slot_idx, 1]))

    @pl.when(step == 0)
    def _():
        x_copy(0).start()
        for cp in w_copies(0, 0):
            cp.start()

    for prio, cp in enumerate(w_copies(jnp.minimum(step + 1, last_step), 1 - slot)):
        cp.start(priority=prio)
    for cp in w_copies(step, slot):
        cp.wait()

    @pl.when(j == 0)
    def _():
        x_copy(i).wait()
        x = xbuf_ref[...]
        hn_ref[...] = _rms(x, g_ref[...]).astype(BF16)
        o_ref[...] = x

    @pl.when((j == 1) & (i < last_tile))
    def _():
        x_copy(i + 1).start()

    def half_step(hn):
        gate = _dot(hn, wgbuf_ref[slot].astype(BF16))
        up = _dot(hn, wubuf_ref[slot].astype(BF16))
        act = (gate * jax.nn.sigmoid(gate) * (0.5 * up)).astype(BF16)
        return _dot(act, wd_ref[...].astype(BF16))

    @pl.when(i < last_tile)
    def _():
        o_ref[...] += half_step(hn_ref[...])

    @pl.when(i == last_tile)
    def _():
        @pl.when(j == 0)
        def _():
            xs = xs_ref[...]
            hns_ref[...] = _rms(xs, g_ref[...]).astype(BF16)
            os_ref[...] = xs

        out = half_step(jnp.concatenate([hn_ref[...], hns_ref[...]], axis=0))
        o_ref[...] += out[:tm]
        os_ref[...] += out[tm:]

    @pl.when(step == last_step)
    def _():
        for cp in w_copies(last_step, 1 - slot):
            cp.wait()


def _ffn(x, xs, g, wg, wu, wd, layer):
    t, d = x.shape
    ts = xs.shape[0]
    f = wg.shape[2]
    tm, tf = FFN_TOKEN_TILE, FFN_FF_TILE
    assert f % tf == 0 and f // tf >= 2 and t % tm == 0
    return pl.pallas_call(
        functools.partial(_ffn_kernel, tm=tm, tf=tf, layer=layer),
        grid=(t // tm, f // tf),
        in_specs=[
            pl.BlockSpec(memory_space=pl.ANY),
            pl.BlockSpec((ts, d), lambda i, j: (0, 0)),
            pl.BlockSpec((1, d), lambda i, j: (0, 0)),
            pl.BlockSpec(memory_space=pl.ANY),
            pl.BlockSpec(memory_space=pl.ANY),
            pl.BlockSpec((None, tf, d), lambda i, j: (layer, j, 0)),
        ],
        out_specs=[pl.BlockSpec((tm, d), lambda i, j: (i, 0)),
                   pl.BlockSpec((ts, d), lambda i, j: (0, 0))],
        out_shape=[jax.ShapeDtypeStruct((t, d), F32), jax.ShapeDtypeStruct((ts, d), F32)],
        scratch_shapes=[pltpu.VMEM((tm, d), F32), pltpu.VMEM((tm, d), BF16),
                        pltpu.VMEM((ts, d), BF16),
                        pltpu.VMEM((2, d, tf), F32), pltpu.VMEM((2, d, tf), F32),
                        pltpu.SemaphoreType.DMA(()), pltpu.SemaphoreType.DMA((2, 2))],
        compiler_params=_params("arbitrary", "arbitrary"),
        name="ffn_half",
    )(x, xs, g, wg, wu, wd)


def _conv_taps(u, uext_ref, tm):
    uext_ref[pl.ds(SUBLANES, tm), :] = u
    um1 = uext_ref[pl.ds(SUBLANES - 1, tm), :]
    um2 = uext_ref[pl.ds(SUBLANES - 2, tm), :]
    return um1, um2


def _mixin_kernel(x_ref, g_ref, w_ref, wf_ref, bf_ref, cw_ref, gc_ref, xs_ref, s1_ref, s2_ref,
                  zc_ref, q_ref, k_ref, v_ref, kb_ref, vb_ref, lf_ref, lfrep_ref, cs_ref,
                  zcs_ref, qs_ref, ks_ref, vs_ref, us_ref, lfs_ref, uext_ref, uexts_ref,
                  *, tm, tiles_per_seq, n_heads, dec_seq, scale, sample_scale):
    i = pl.program_id(0)
    c = cw_ref.shape[1]

    def project(x):
        hn = _rms(x, g_ref[...]).astype(BF16)
        lf = _log_sigmoid(_dot(hn, wf_ref[...]) + bf_ref[...])
        return lf, lambda grp: _dot(hn, w_ref[:, grp * c:(grp + 1) * c])

    def conv_branch(cb, u, um1, um2):
        y = cb * (cw_ref[0:1, :] * um2 + cw_ref[1:2, :] * um1 + cw_ref[2:3, :] * u)
        return _rms(y, gc_ref[...]).astype(BF16)

    lf, proj = project(x_ref[...])
    lf_ref[...] = lf[:, :n_heads]
    lfrep_ref[...] = lf
    cb = proj(0)
    u = proj(1) * proj(2)
    head = jnp.where(i % tiles_per_seq == 0, 0.0, uext_ref[pl.ds(0, SUBLANES), :])
    uext_ref[pl.ds(0, SUBLANES), :] = head
    um1, um2 = _conv_taps(u, uext_ref, tm)
    zc_ref[...] = conv_branch(cb, u, um1, um2)
    cs_ref[0] = u[tm - (CONV_W - 1):, :]
    uext_ref[pl.ds(0, SUBLANES), :] = u[tm - SUBLANES:, :]
    q_ref[...] = (proj(3) * scale).astype(BF16)
    z = proj(4)
    k_ref[...] = z
    kb_ref[...] = z.astype(BF16)
    z = proj(5)
    v_ref[...] = z
    vb_ref[...] = z.astype(BF16)

    @pl.when(i == pl.num_programs(0) - 1)
    def _():
        ts = xs_ref.shape[0]
        lf, proj = project(xs_ref[...])
        lfs_ref[...] = lf[:, :n_heads]
        cb = proj(0)
        u = proj(1) * proj(2)
        uexts_ref[pl.ds(0, SUBLANES), :] = jnp.zeros((SUBLANES, c), F32)
        um1, um2 = _conv_taps(u, uexts_ref, ts)
        step = lax.broadcasted_iota(jnp.int32, u.shape, 0) % dec_seq
        um1 = jnp.where(step >= 1, um1, s1_ref[...])
        um2 = jnp.where(step >= 2, um2, s2_ref[...])
        zcs_ref[...] = conv_branch(cb, u, um1, um2)
        us_ref[...] = u
        qs_ref[...] = proj(3) * sample_scale
        ks_ref[...] = proj(4)
        vs_ref[...] = proj(5)


def _mixin(x, xs, g, w_main, w_f, b_f, cw, gc, s1, s2, seq, dec_seq, n_heads, scale,
           sample_scale):
    t, d = x.shape
    ts = xs.shape[0]
    c = cw.shape[1]
    tm = MIXIN_TOKEN_TILE
    row = lambda i: (i, 0)
    fixed = lambda i: (0, 0)
    once = lambda shape: pl.BlockSpec(shape, fixed, pipeline_mode=pl.Buffered(1))
    kern = functools.partial(_mixin_kernel, tm=tm, tiles_per_seq=seq // tm, n_heads=n_heads,
                             dec_seq=dec_seq, scale=scale, sample_scale=sample_scale)
    return pl.pallas_call(
        kern,
        grid=(t // tm,),
        in_specs=[
            pl.BlockSpec((tm, d), row),
            pl.BlockSpec((1, d), fixed),
            once(w_main.shape),
            pl.BlockSpec((d, LANES), fixed),
            pl.BlockSpec((1, LANES), fixed),
            pl.BlockSpec((CONV_W, c), fixed),
            pl.BlockSpec((1, c), fixed),
            once((ts, d)),
            once((ts, c)),
            once((ts, c)),
        ],
        out_specs=[
            pl.BlockSpec((tm, c), row),
            pl.BlockSpec((tm, c), row),
            pl.BlockSpec((tm, c), row),
            pl.BlockSpec((tm, c), row),
            pl.BlockSpec((tm, c), row),
            pl.BlockSpec((tm, c), row),
            pl.BlockSpec((tm, n_heads), row),
            pl.BlockSpec((tm, LANES), row),
            pl.BlockSpec((1, CONV_W - 1, c), lambda i: (i // (seq // tm), 0, 0)),
            pl.BlockSpec((ts, c), fixed),
            pl.BlockSpec((ts, c), fixed),
            pl.BlockSpec((ts, c), fixed),
            pl.BlockSpec((ts, c), fixed),
            pl.BlockSpec((ts, c), fixed),
            pl.BlockSpec((ts, n_heads), fixed),
        ],
        out_shape=[
            jax.ShapeDtypeStruct((t, c), BF16),
            jax.ShapeDtypeStruct((t, c), BF16),
            jax.ShapeDtypeStruct((t, c), F32),
            jax.ShapeDtypeStruct((t, c), F32),
            jax.ShapeDtypeStruct((t, c), BF16),
            jax.ShapeDtypeStruct((t, c), BF16),
            jax.ShapeDtypeStruct((t, n_heads), F32),
            jax.ShapeDtypeStruct((t, LANES), F32),
            jax.ShapeDtypeStruct((t // seq, CONV_W - 1, c), F32),
            jax.ShapeDtypeStruct((ts, c), BF16),
            jax.ShapeDtypeStruct((ts, c), F32),
            jax.ShapeDtypeStruct((ts, c), F32),
            jax.ShapeDtypeStruct((ts, c), F32),
            jax.ShapeDtypeStruct((ts, c), F32),
            jax.ShapeDtypeStruct((ts, n_heads), F32),
        ],
        scratch_shapes=[pltpu.VMEM((tm + SUBLANES, c), F32), pltpu.VMEM((ts + SUBLANES, c), F32)],
        compiler_params=_params("arbitrary"),
        name="mixer_in",
    )(x, g, w_main, w_f, b_f, cw, gc, xs, s1, s2)


def _split3(x):
    hi = x.astype(BF16)
    r1 = x - hi.astype(F32)
    mid = r1.astype(BF16)
    lo = (r1 - mid.astype(F32)).astype(BF16)
    return hi, mid, lo


def _attn_prompt_kernel(q_ref, k_ref, v_ref, lfrep_ref, ga_ref, za_ref, aug_ref, o_ref,
                        *, blk, n_heads):
    i = pl.program_id(1)
    seq = k_ref.shape[0]
    lane = lax.broadcasted_iota(jnp.int32, (blk, LANES), 1)
    row = lax.broadcasted_iota(jnp.int32, (blk, blk), 0)
    col = lax.broadcasted_iota(jnp.int32, (blk, blk), 1)
    causal = col <= row

    @pl.when(i == 0)
    def _():
        tri = jnp.where(causal, 1.0, 0.0).astype(BF16)
        carry = jnp.zeros((1, LANES), F32)
        for jb in range(seq // blk):
            rows = slice(jb * blk, (jb + 1) * blk)
            hi, mid, lo = _split3(lfrep_ref[rows, :])
            f = _dot(tri, hi) + _dot(tri, mid) + _dot(tri, lo) + carry
            carry = f[blk - 1:blk, :]
            hi, mid, lo = _split3(f * (-LOG2E))
            zero = jnp.zeros_like(hi)
            aug_ref[rows, :] = jnp.where(
                lane < n_heads, hi,
                jnp.where(lane < 2 * n_heads, mid, jnp.where(lane < 3 * n_heads, lo, zero)))

    def head_out(h, nblk):
        hs = slice(h * HEAD_DIM, (h + 1) * HEAD_DIM)
        ones = jnp.where((lane % n_heads == h) & (lane < 3 * n_heads), 1.0, 0.0).astype(BF16)
        qa = jnp.concatenate([q_ref[:, hs], ones], axis=1)
        scores = []
        for jb in range(nblk):
            rows = slice(jb * blk, (jb + 1) * blk)
            sj = _dot_nt(qa, jnp.concatenate([k_ref[rows, hs], aug_ref[rows, :]], axis=1))
            scores.append(jnp.where(causal, sj, MASK_VALUE) if jb == nblk - 1 else sj)
        mx = scores[0]
        for sj in scores[1:]:
            mx = jnp.maximum(mx, sj)
        m = jnp.max(mx, axis=-1, keepdims=True)
        acc = psum = None
        for jb, sj in enumerate(scores):
            p = jnp.exp2(sj - m)
            o = _dot(p.astype(BF16), v_ref[jb * blk:(jb + 1) * blk, hs])
            psum = p if psum is None else psum + p
            acc = o if acc is None else acc + o
        return acc / jnp.sum(psum, axis=-1, keepdims=True)

    for nblk in range(1, seq // blk + 1):
        @pl.when(i == nblk - 1)
        def _():
            for h in range(n_heads):
                o_ref[:, h * HEAD_DIM:(h + 1) * HEAD_DIM] = head_out(h, nblk)

    za_ref[...] = _rms(o_ref[...], ga_ref[...]).astype(BF16)


def _attn_prompt(q, kb, vb, lfrep, ga, seq, n_heads):
    t, c = q.shape
    b = t // seq
    blk = ATTN_BLOCK
    nq = seq // blk
    kern = functools.partial(_attn_prompt_kernel, blk=blk, n_heads=n_heads)
    return pl.pallas_call(
        kern,
        grid=(b, nq),
        in_specs=[
            pl.BlockSpec((blk, c), lambda bi, i: (bi * nq + i, 0)),
            pl.BlockSpec((seq, c), lambda bi, i: (bi, 0)),
            pl.BlockSpec((seq, c), lambda bi, i: (bi, 0)),
            pl.BlockSpec((seq, LANES), lambda bi, i: (bi, 0)),
            pl.BlockSpec((1, c), lambda bi, i: (0, 0)),
        ],
        out_specs=pl.BlockSpec((blk, c), lambda bi, i: (bi * nq + i, 0)),
        out_shape=jax.ShapeDtypeStruct((t, c), BF16),
        scratch_shapes=[pltpu.VMEM((seq, LANES), BF16), pltpu.VMEM((blk, c), F32)],
        compiler_params=_params("arbitrary", "arbitrary"),
        name="fox_prompt",
    )(q, kb, vb, lfrep, ga)


def _split3_dot(x, w):
    hi, mid, lo = _split3(x)
    return _dot(hi, w) + _dot(mid, w) + _dot(lo, w)


def _select_sum_kernel(x_ref, w_ref, o_ref):
    o_ref[...] = _split3_dot(x_ref[...], w_ref[...])


def _select_sum(x, w, tm):
    rows, k = x.shape
    n = w.shape[1]
    return pl.pallas_call(
        _select_sum_kernel,
        grid=(rows // tm,),
        in_specs=[pl.BlockSpec((tm, k), lambda i: (i, 0)), pl.BlockSpec((k, n), lambda i: (0, 0))],
        out_specs=pl.BlockSpec((tm, n), lambda i: (i, 0)),
        out_shape=jax.ShapeDtypeStruct((rows, n), F32),
        compiler_params=_params("parallel"),
        name="head_major_prefix_sum",
    )(x, w)


def _prefix_matrix(n_steps, n_heads, width):
    r = jnp.arange(n_steps * n_heads)
    cidx = jnp.arange(n_heads * width)
    t, h = r // n_heads, r % n_heads
    h2, t2 = cidx // width, cidx % width
    w = (h[:, None] == h2[None, :]) & (t[:, None] <= t2[None, :]) & (t2[None, :] < n_steps)
    return w.astype(BF16)


def _attn_sample_kernel(pt_ref, q_ref, kn_ref, vn_ref, fn_ref, ga_ref, ck_hbm, cv_hbm, pf_hbm,
                        za_ref, kbuf_ref, vbuf_ref, pfbuf_ref, sem, qbd_ref, kb_ref, vb_ref,
                        m_ref, l_ref, acc_ref, run_ref, *, pages, groups, n_steps, layer, n_heads,
                        dec_seq):
    g = pl.program_id(1)
    step = pl.program_id(0) * groups + g
    slot = step % 2

    def page_copies(step_idx, slot_idx):
        bb = step_idx // groups
        first = (step_idx % groups) * pages
        out = []
        for r in range(pages):
            page = pt_ref[bb, first + r]
            out.append(pltpu.make_async_copy(ck_hbm.at[layer, page], kbuf_ref.at[slot_idx, r],
                                             sem.at[slot_idx, 0]))
            out.append(pltpu.make_async_copy(cv_hbm.at[layer, page], vbuf_ref.at[slot_idx, r],
                                             sem.at[slot_idx, 1]))
            out.append(pltpu.make_async_copy(pf_hbm.at[page], pfbuf_ref.at[slot_idx, r],
                                             sem.at[slot_idx, 2]))
        return out

    @pl.when(step == 0)
    def _():
        for cp in page_copies(0, 0):
            cp.start()

    for n, cp in enumerate(page_copies(jnp.minimum(step + 1, n_steps - 1), 1 - slot)):
        cp.start(priority=1 if n % 3 == 1 else 0)

    for cp in page_copies(step, slot):
        cp.wait()

    rows = dec_seq * n_heads
    c = n_heads * HEAD_DIM
    head_of_row = lax.broadcasted_iota(jnp.int32, (n_heads, c), 0)
    head_of_col = lax.broadcasted_iota(jnp.int32, (n_heads, c), 1) // HEAD_DIM
    diag = head_of_row == head_of_col

    @pl.when(g == 0)
    def _():
        for t in range(dec_seq):
            qt = jnp.broadcast_to(q_ref[0, t:t + 1, :], (n_heads, c))
            qbd_ref[t * n_heads:(t + 1) * n_heads, :] = jnp.where(diag, qt, 0.0).astype(BF16)
        m_ref[...] = jnp.full(m_ref.shape, MASK_VALUE, F32)
        l_ref[...] = jnp.zeros(l_ref.shape, F32)
        acc_ref[...] = jnp.zeros(acc_ref.shape, F32)
        run_ref[...] = jnp.zeros(run_ref.shape, F32)

    def online_update(s, v_bf):
        m_old = m_ref[...]
        m_new = jnp.maximum(m_old, jnp.max(s, axis=-1, keepdims=True))
        alpha = jnp.exp(m_old - m_new)
        p = jnp.exp(s - m_new)
        l_ref[...] = alpha * l_ref[...] + jnp.sum(p, axis=-1, keepdims=True)
        acc_ref[...] = alpha * acc_ref[...] + _dot(p.astype(BF16), v_bf)
        m_ref[...] = m_new

    run = run_ref[...]
    bias_parts = []
    for pg in range(pages):
        toks = slice(pg * PAGE_SIZE, (pg + 1) * PAGE_SIZE)
        for h in range(n_heads):
            hs = slice(h * HEAD_DIM, (h + 1) * HEAD_DIM)
            head_rows = pl.ds(h, PAGE_SIZE, stride=n_heads)
            kb_ref[toks, hs] = kbuf_ref[slot, pg, head_rows, :].astype(BF16)
            vb_ref[toks, hs] = vbuf_ref[slot, pg, head_rows, :].astype(BF16)
        pf = pfbuf_ref[slot, pg]
        bias_parts.append(run + pf)
        run = run + jnp.broadcast_to(pf[:, PAGE_SIZE - 1:PAGE_SIZE], pf.shape)
    run_ref[...] = run
    bias = jnp.concatenate(bias_parts, axis=1)
    bias = jnp.concatenate([bias] * dec_seq, axis=0)
    s = _dot_nt(qbd_ref[...], kb_ref[...]) - bias
    online_update(s, vb_ref[...])

    @pl.when(g == pl.num_programs(1) - 1)
    def _():
        pad = jnp.zeros((PAGE_SIZE - SUBLANES, c), F32)
        kn = jnp.concatenate([kn_ref[0], pad], axis=0).astype(BF16)
        vn = jnp.concatenate([vn_ref[0], pad], axis=0).astype(BF16)
        bias_n = jnp.concatenate([run + fn_ref[0]] * dec_seq, axis=0)
        s_n = _dot_nt(qbd_ref[...], kn) - bias_n
        q_step = lax.broadcasted_iota(jnp.int32, (rows, PAGE_SIZE), 0) // n_heads
        k_step = lax.broadcasted_iota(jnp.int32, (rows, PAGE_SIZE), 1)
        s_n = jnp.where(k_step <= q_step, s_n, MASK_VALUE)
        online_update(s_n, vn)
        o = acc_ref[...] / l_ref[...]
        for t in range(dec_seq):
            ot = jnp.where(diag, o[t * n_heads:(t + 1) * n_heads, :], 0.0)
            ot = jnp.sum(ot, axis=0, keepdims=True)
            za_ref[0, t:t + 1, :] = _rms(ot, ga_ref[...])

    @pl.when(step == n_steps - 1)
    def _():
        for cp in page_copies(n_steps - 1, 1 - slot):
            cp.wait()


def _attn_sample(page_table, q, k_new, v_new, f_new, ga, cache_k, cache_v, pfx, layer, n_heads,
                 dec_seq):
    nb, n_pages = page_table.shape
    c = n_heads * HEAD_DIM
    pages = PAGES_PER_STEP
    groups = n_pages // pages
    rows = dec_seq * n_heads
    per_b = lambda b, g, pt: (b, 0, 0)
    kern = functools.partial(_attn_sample_kernel, pages=pages, groups=groups, n_steps=nb * groups,
                             layer=layer, n_heads=n_heads, dec_seq=dec_seq)
    grid_spec = pltpu.PrefetchScalarGridSpec(
        num_scalar_prefetch=1,
        grid=(nb, groups),
        in_specs=[
            pl.BlockSpec((1, dec_seq, c), per_b),
            pl.BlockSpec((1, SUBLANES, c), per_b),
            pl.BlockSpec((1, SUBLANES, c), per_b),
            pl.BlockSpec((1, n_heads, LANES), per_b),
            pl.BlockSpec((1, c), lambda b, g, pt: (0, 0)),
            pl.BlockSpec(memory_space=pl.ANY),
            pl.BlockSpec(memory_space=pl.ANY),
            pl.BlockSpec(memory_space=pl.ANY),
        ],
        out_specs=pl.BlockSpec((1, dec_seq, c), per_b),
        scratch_shapes=[
            pltpu.VMEM((2, pages, PAGE_SIZE * n_heads, HEAD_DIM), F32),
            pltpu.VMEM((2, pages, PAGE_SIZE * n_heads, HEAD_DIM), F32),
            pltpu.VMEM((2, pages, n_heads, LANES), F32),
            pltpu.SemaphoreType.DMA((2, 3)),
            pltpu.VMEM((rows, c), BF16),
            pltpu.VMEM((pages * PAGE_SIZE, c), BF16),
            pltpu.VMEM((pages * PAGE_SIZE, c), BF16),
            pltpu.VMEM((rows, 1), F32),
            pltpu.VMEM((rows, 1), F32),
            pltpu.VMEM((rows, c), F32),
            pltpu.VMEM((n_heads, LANES), F32),
        ],
    )
    return pl.pallas_call(
        kern,
        grid_spec=grid_spec,
        out_shape=jax.ShapeDtypeStruct((nb, dec_seq, c), F32),
        compiler_params=_params("arbitrary", "arbitrary"),
        name="fox_sample",
    )(page_table, q, k_new, v_new, f_new, ga, cache_k, cache_v, pfx)


def _resident(block_shape, index_map):
    return pl.BlockSpec(block_shape, index_map, pipeline_mode=pl.Buffered(1))


def _mixout_kernel(h_ref, zc_ref, za_ref, hs_ref, zcs_ref, zas_ref, wc_ref, wa_ref, o_ref, os_ref,
                   wcb_ref, wab_ref):
    @pl.when(pl.program_id(0) == 0)
    def _():
        wcb_ref[...] = wc_ref[...].astype(BF16)
        wab_ref[...] = wa_ref[...].astype(BF16)

    def project(h, zc, za):
        return h + _dot(zc.astype(BF16), wcb_ref[...]) + _dot(za.astype(BF16), wab_ref[...])

    o_ref[...] = project(h_ref[...], zc_ref[...], za_ref[...])

    @pl.when(pl.program_id(0) == pl.num_programs(0) - 1)
    def _():
        os_ref[...] = project(hs_ref[...], zcs_ref[...], zas_ref[...])


def _mixout(h, zc, za, hs, zcs, zas, w_out, layer, tm):
    t, d = h.shape
    ts = hs.shape[0]
    c = zc.shape[1]
    assert za.shape[1] == c and w_out.shape[1] == 2 * c
    row = lambda i: (i, 0)
    fixed = lambda i: (0, 0)
    return pl.pallas_call(
        _mixout_kernel,
        grid=(t // tm,),
        in_specs=[pl.BlockSpec((tm, d), row), pl.BlockSpec((tm, c), row), pl.BlockSpec((tm, c), row),
                  _resident((ts, d), fixed), _resident((ts, c), fixed), _resident((ts, c), fixed),
                  _resident((None, c, d), lambda i: (layer, 0, 0)),
                  _resident((None, c, d), lambda i: (layer, 1, 0))],
        out_specs=[pl.BlockSpec((tm, d), row), pl.BlockSpec((ts, d), fixed)],
        out_shape=[jax.ShapeDtypeStruct((t, d), F32), jax.ShapeDtypeStruct((ts, d), F32)],
        scratch_shapes=[pltpu.VMEM((c, d), BF16), pltpu.VMEM((c, d), BF16)],
        compiler_params=_params("arbitrary"),
        name="mixer_out",
    )(h, zc, za, hs, zcs, zas, w_out, w_out)


def _ple_kernel(h_ref, p_ref, hs_ref, ps_ref, g_ref, wg_ref, wp_ref, gf_ref, o_ref, os_ref,
                wgb_ref, wpb_ref, *, final):
    @pl.when(pl.program_id(0) == 0)
    def _():
        wgb_ref[...] = wg_ref[...].astype(BF16)
        wpb_ref[...] = wp_ref[...].astype(BF16)

    def embed(h, p):
        hn = _rms(h, g_ref[...]).astype(BF16)
        gate = jax.nn.sigmoid(_dot(hn, wgb_ref[...]))
        out = h + gate * _dot(p.astype(BF16), wpb_ref[...])
        return _rms(out, gf_ref[...]) if final else out

    o_ref[...] = embed(h_ref[...], p_ref[...])

    @pl.when(pl.program_id(0) == pl.num_programs(0) - 1)
    def _():
        os_ref[...] = embed(hs_ref[...], ps_ref[...])


def _ple(h, p, hs, ps, g, wg, wp, gf, layer, final, tm):
    t, d = h.shape
    ts = hs.shape[0]
    pd = p.shape[1]
    row = lambda i: (i, 0)
    fixed = lambda i: (0, 0)
    return pl.pallas_call(
        functools.partial(_ple_kernel, final=final),
        grid=(t // tm,),
        in_specs=[pl.BlockSpec((tm, d), row), pl.BlockSpec((tm, pd), row),
                  _resident((ts, d), fixed), _resident((ts, pd), fixed),
                  pl.BlockSpec((1, d), fixed),
                  _resident((None, d, d), lambda i: (layer, 0, 0)),
                  _resident((None, pd, d), lambda i: (layer, 0, 0)),
                  pl.BlockSpec((1, d), fixed)],
        out_specs=[pl.BlockSpec((tm, d), row), pl.BlockSpec((ts, d), fixed)],
        out_shape=[jax.ShapeDtypeStruct((t, d), F32), jax.ShapeDtypeStruct((ts, d), F32)],
        scratch_shapes=[pltpu.VMEM((d, d), BF16), pltpu.VMEM((pd, d), BF16)],
        compiler_params=_params("arbitrary"),
        name="ple",
    )(h, p, hs, ps, g, wg, wp, gf)


def kernel(x_prompt, x_sample, p_prompt, p_sample, cache_k, cache_v, cache_logf, state_conv, page_table, norm_ffn1, w_ffn1_gate, w_ffn1_up, w_ffn1_down, norm_mix, w_in, b_f, conv_w, norm_conv_out, norm_attn_out, w_out, norm_ffn2, w_ffn2_gate, w_ffn2_up, w_ffn2_down, norm_ple, w_ple_gate, w_ple_proj, norm_final):
    batch, seq, d = x_prompt.shape
    dec_batch, dec_seq, _ = x_sample.shape
    depth = w_in.shape[0]
    n_heads = b_f.shape[1]
    conv_dim = conv_w.shape[2]
    attn_dim = n_heads * HEAD_DIM
    n_pool = cache_k.shape[1]
    scale = HEAD_DIM ** -0.5
    n_main = 3 * conv_dim + 3 * attn_dim
    assert conv_dim == attn_dim and dec_seq >= CONV_W - 1 and dec_seq <= SUBLANES
    tp, ts = batch * seq, dec_batch * dec_seq

    hp = x_prompt.reshape(tp, d)
    hs = x_sample.reshape(ts, d)
    row2 = lambda a: a.reshape(1, -1)
    page_prefix_w = _prefix_matrix(PAGE_SIZE, n_heads, PAGE_SIZE)
    new_prefix_w = _prefix_matrix(dec_seq, n_heads, LANES)

    outs = [[] for _ in range(8)]
    for l in range(depth):
        w_main = w_in[l].astype(BF16)
        w_f = jnp.tile(w_in[l, :, n_main:], (1, LANES // n_heads)).astype(BF16)
        g1, gm, g2, gp = row2(norm_ffn1[l]), row2(norm_mix[l]), row2(norm_ffn2[l]), row2(norm_ple[l])
        gc, ga = row2(norm_conv_out[l]), row2(norm_attn_out[l])
        bf = row2(jnp.tile(b_f[l], LANES // n_heads))
        cw = conv_w[l]
        last = l == depth - 1
        gfin = row2(norm_final)

        hp, hs = _ffn(hp, hs, g1, w_ffn1_gate, w_ffn1_up, w_ffn1_down, l)

        st = state_conv[l]
        zero = jnp.zeros((dec_batch, dec_seq - 1, conv_dim), F32)
        s1 = jnp.concatenate([st[:, 1:2], zero], axis=1).reshape(ts, conv_dim)
        s2 = jnp.concatenate([st, zero[:, 1:]], axis=1).reshape(ts, conv_dim)
        (zc_p, q_p, k_p, v_p, kb_p, vb_p, lf_p, lfrep_p, cs_p, zc_s, q_s, k_s, v_s, u_s, lf_s) = _mixin(
            hp, hs, gm, w_main, w_f, bf, cw, gc, s1, s2, seq, dec_seq, n_heads, scale * LOG2E, scale)

        za_p = _attn_prompt(q_p, kb_p, vb_p, lfrep_p, ga, seq, n_heads)

        pfx = _select_sum(cache_logf[l].reshape(n_pool, PAGE_SIZE * n_heads), page_prefix_w, 512)
        pfx = pfx.reshape(n_pool, n_heads, PAGE_SIZE)
        f_new = _select_sum(lf_s.reshape(dec_batch, dec_seq * n_heads), new_prefix_w, dec_batch)
        f_new = f_new.reshape(dec_batch, n_heads, LANES)
        pad8 = lambda a: jnp.pad(a.reshape(dec_batch, dec_seq, attn_dim),
                                 ((0, 0), (0, SUBLANES - dec_seq), (0, 0)))
        za_s = _attn_sample(page_table, q_s.reshape(dec_batch, dec_seq, attn_dim), pad8(k_s),
                            pad8(v_s), f_new, ga,
                            cache_k.reshape(depth, n_pool, PAGE_SIZE * n_heads, HEAD_DIM),
                            cache_v.reshape(depth, n_pool, PAGE_SIZE * n_heads, HEAD_DIM),
                            pfx, l, n_heads, dec_seq)

        hp, hs = _mixout(hp, zc_p, za_p, hs, zc_s, za_s.reshape(ts, attn_dim), w_out, l, TOKEN_TILE)
        hp, hs = _ffn(hp, hs, g2, w_ffn2_gate, w_ffn2_up, w_ffn2_down, l)
        hp, hs = _ple(hp, p_prompt[l].reshape(tp, -1), hs, p_sample[l].reshape(ts, -1), gp,
                      w_ple_gate, w_ple_proj, gfin, l, last, TOKEN_TILE)

        outs[0].append(k_p.reshape(batch, seq, n_heads, HEAD_DIM))
        outs[1].append(v_p.reshape(batch, seq, n_heads, HEAD_DIM))
        outs[2].append(lf_p.reshape(batch, seq, n_heads))
        outs[3].append(cs_p)
        outs[4].append(k_s.reshape(dec_batch, dec_seq, n_heads, HEAD_DIM))
        outs[5].append(v_s.reshape(dec_batch, dec_seq, n_heads, HEAD_DIM))
        outs[6].append(lf_s.reshape(dec_batch, dec_seq, n_heads))
        outs[7].append(u_s.reshape(dec_batch, dec_seq, conv_dim)[:, dec_seq - (CONV_W - 1):])

    return (hp.reshape(batch, seq, d), hs.reshape(dec_batch, dec_seq, d),
            *(jnp.stack(o) for o in outs))
```

```python
import functools

import jax
import jax.numpy as jnp
from jax import lax
from jax.experimental import pallas as pl
from jax.experimental.pallas import tpu as pltpu

F32 = jnp.float32
BF16 = jnp.bfloat16

EPS = 1e-6
HEAD_DIM = 128
PAGE_SIZE = 128
CONV_W = 3
LANES = 128
SUBLANES = 8
MASK_VALUE = -1e30
VMEM_LIMIT = 56 * 1024 * 1024

TOKEN_TILE = 512
MIXIN_TOKEN_TILE = 256
FFN_TOKEN_TILE = 1024
FFN_FF_TILE = 256
ATTN_BLOCK = 256
LOG2E = 1.4426950408889634
PAGES_PER_STEP = 16


def _params(*sem):
    return pltpu.CompilerParams(dimension_semantics=sem, vmem_limit_bytes=VMEM_LIMIT)


def _rms(x, g):
    ms = jnp.mean(x * x, axis=-1, keepdims=True)
    return x * lax.rsqrt(ms + EPS) * g


def _dot(a, b):
    return jnp.dot(a, b, preferred_element_type=F32)


def _dot_nt(a, b):
    return lax.dot_general(a, b, (((1,), (1,)), ((), ())), preferred_element_type=F32)


def _log_sigmoid(x):
    return jnp.minimum(x, 0.0) - jnp.log1p(jnp.exp(-jnp.abs(x)))


def _ffn_kernel(x_hbm, xs_ref, g_ref, wg_ref, wu_ref, wd_ref, o_ref, os_ref,
                xbuf_ref, hn_ref, hns_ref, sem, *, tm):
    i = pl.program_id(0)
    j = pl.program_id(1)
    last_tile = pl.num_programs(0) - 1

    def x_copy(tile):
        return pltpu.make_async_copy(x_hbm.at[pl.ds(tile * tm, tm), :], xbuf_ref, sem)

    @pl.when((i == 0) & (j == 0))
    def _():
        x_copy(0).start()

    @pl.when(j == 0)
    def _():
        x_copy(i).wait()
        x = xbuf_ref[...]
        hn_ref[...] = _rms(x, g_ref[...]).astype(BF16)
        o_ref[...] = x

    @pl.when((j == 1) & (i < last_tile))
    def _():
        x_copy(i + 1).start()

    def half_step(hn):
        gate = _dot(hn, wg_ref[...].astype(BF16))
        up = _dot(hn, wu_ref[...].astype(BF16))
        act = (gate * jax.nn.sigmoid(gate) * (0.5 * up)).astype(BF16)
        return _dot(act, wd_ref[...].astype(BF16))

    @pl.when(i < last_tile)
    def _():
        o_ref[...] += half_step(hn_ref[...])

    @pl.when(i == last_tile)
    def _():
        @pl.when(j == 0)
        def _():
            xs = xs_ref[...]
            hns_ref[...] = _rms(xs, g_ref[...]).astype(BF16)
            os_ref[...] = xs

        out = half_step(jnp.concatenate([hn_ref[...], hns_ref[...]], axis=0))
        o_ref[...] += out[:tm]
        os_ref[...] += out[tm:]


def _ffn(x, xs, g, wg, wu, wd, layer):
    t, d = x.shape
    ts = xs.shape[0]
    f = wg.shape[2]
    tm, tf = FFN_TOKEN_TILE, FFN_FF_TILE
    assert f % tf == 0 and f // tf >= 2 and t % tm == 0
    return pl.pallas_call(
        functools.partial(_ffn_kernel, tm=tm),
        grid=(t // tm, f // tf),
        in_specs=[
            pl.BlockSpec(memory_space=pl.ANY),
            pl.BlockSpec((ts, d), lambda i, j: (0, 0)),
            pl.BlockSpec((1, d), lambda i, j: (0, 0)),
            pl.BlockSpec((None, d, tf), lambda i, j: (layer, 0, j)),
            pl.BlockSpec((None, d, tf), lambda i, j: (layer, 0, j)),
            pl.BlockSpec((None, tf, d), lambda i, j: (layer, j, 0)),
        ],
        out_specs=[pl.BlockSpec((tm, d), lambda i, j: (i, 0)),
                   pl.BlockSpec((ts, d), lambda i, j: (0, 0))],
        out_shape=[jax.ShapeDtypeStruct((t, d), F32), jax.ShapeDtypeStruct((ts, d), F32)],
        scratch_shapes=[pltpu.VMEM((tm, d), F32), pltpu.VMEM((tm, d), BF16),
                        pltpu.VMEM((ts, d), BF16), pltpu.SemaphoreType.DMA(())],
        compiler_params=_params("arbitrary", "arbitrary"),
        name="ffn_half",
    )(x, xs, g, wg, wu, wd)


def _conv_taps(u, uext_ref, tm):
    uext_ref[pl.ds(SUBLANES, tm), :] = u
    um1 = uext_ref[pl.ds(SUBLANES - 1, tm), :]
    um2 = uext_ref[pl.ds(SUBLANES - 2, tm), :]
    return um1, um2


def _mixin_kernel(x_ref, g_ref, w_ref, wf_ref, bf_ref, cw_ref, gc_ref, xs_ref, s1_ref, s2_ref,
                  zc_ref, q_ref, k_ref, v_ref, kb_ref, vb_ref, lf_ref, lfrep_ref, cs_ref,
                  zcs_ref, qs_ref, ks_ref, vs_ref, us_ref, lfs_ref, uext_ref, uexts_ref,
                  *, tm, tiles_per_seq, n_heads, dec_seq, scale, sample_scale):
    i = pl.program_id(0)
    c = cw_ref.shape[1]

    def project(x):
        hn = _rms(x, g_ref[...]).astype(BF16)
        lf = _log_sigmoid(_dot(hn, wf_ref[...]) + bf_ref[...])
        return lf, lambda grp: _dot(hn, w_ref[:, grp * c:(grp + 1) * c])

    def conv_branch(cb, u, um1, um2):
        y = cb * (cw_ref[0:1, :] * um2 + cw_ref[1:2, :] * um1 + cw_ref[2:3, :] * u)
        return _rms(y, gc_ref[...]).astype(BF16)

    lf, proj = project(x_ref[...])
    lf_ref[...] = lf[:, :n_heads]
    lfrep_ref[...] = lf
    cb = proj(0)
    u = proj(1) * proj(2)
    head = jnp.where(i % tiles_per_seq == 0, 0.0, uext_ref[pl.ds(0, SUBLANES), :])
    uext_ref[pl.ds(0, SUBLANES), :] = head
    um1, um2 = _conv_taps(u, uext_ref, tm)
    zc_ref[...] = conv_branch(cb, u, um1, um2)
    cs_ref[0] = u[tm - (CONV_W - 1):, :]
    uext_ref[pl.ds(0, SUBLANES), :] = u[tm - SUBLANES:, :]
    q_ref[...] = (proj(3) * scale).astype(BF16)
    z = proj(4)
    k_ref[...] = z
    kb_ref[...] = z.astype(BF16)
    z = proj(5)
    v_ref[...] = z
    vb_ref[...] = z.astype(BF16)

    @pl.when(i == pl.num_programs(0) - 1)
    def _():
        ts = xs_ref.shape[0]
        lf, proj = project(xs_ref[...])
        lfs_ref[...] = lf[:, :n_heads]
        cb = proj(0)
        u = proj(1) * proj(2)
        uexts_ref[pl.ds(0, SUBLANES), :] = jnp.zeros((SUBLANES, c), F32)
        um1, um2 = _conv_taps(u, uexts_ref, ts)
        step = lax.broadcasted_iota(jnp.int32, u.shape, 0) % dec_seq
        um1 = jnp.where(step >= 1, um1, s1_ref[...])
        um2 = jnp.where(step >= 2, um2, s2_ref[...])
        zcs_ref[...] = conv_branch(cb, u, um1, um2)
        us_ref[...] = u
        qs_ref[...] = proj(3) * sample_scale
        ks_ref[...] = proj(4)
        vs_ref[...] = proj(5)


def _mixin(x, xs, g, w_main, w_f, b_f, cw, gc, s1, s2, seq, dec_seq, n_heads, scale,
           sample_scale):
    t, d = x.shape
    ts = xs.shape[0]
    c = cw.shape[1]
    tm = MIXIN_TOKEN_TILE
    row = lambda i: (i, 0)
    fixed = lambda i: (0, 0)
    once = lambda shape: pl.BlockSpec(shape, fixed, pipeline_mode=pl.Buffered(1))
    kern = functools.partial(_mixin_kernel, tm=tm, tiles_per_seq=seq // tm, n_heads=n_heads,
                             dec_seq=dec_seq, scale=scale, sample_scale=sample_scale)
    return pl.pallas_call(
        kern,
        grid=(t // tm,),
        in_specs=[
            pl.BlockSpec((tm, d), row),
            pl.BlockSpec((1, d), fixed),
            once(w_main.shape),
            pl.BlockSpec((d, LANES), fixed),
            pl.BlockSpec((1, LANES), fixed),
            pl.BlockSpec((CONV_W, c), fixed),
            pl.BlockSpec((1, c), fixed),
            once((ts, d)),
            once((ts, c)),
            once((ts, c)),
        ],
        out_specs=[
            pl.BlockSpec((tm, c), row),
            pl.BlockSpec((tm, c), row),
            pl.BlockSpec((tm, c), row),
            pl.BlockSpec((tm, c), row),
            pl.BlockSpec((tm, c), row),
            pl.BlockSpec((tm, c), row),
            pl.BlockSpec((tm, n_heads), row),
            pl.BlockSpec((tm, LANES), row),
            pl.BlockSpec((1, CONV_W - 1, c), lambda i: (i // (seq // tm), 0, 0)),
            pl.BlockSpec((ts, c), fixed),
            pl.BlockSpec((ts, c), fixed),
            pl.BlockSpec((ts, c), fixed),
            pl.BlockSpec((ts, c), fixed),
            pl.BlockSpec((ts, c), fixed),
            pl.BlockSpec((ts, n_heads), fixed),
        ],
        out_shape=[
            jax.ShapeDtypeStruct((t, c), BF16),
            jax.ShapeDtypeStruct((t, c), BF16),
            jax.ShapeDtypeStruct((t, c), F32),
            jax.ShapeDtypeStruct((t, c), F32),
            jax.ShapeDtypeStruct((t, c), BF16),
            jax.ShapeDtypeStruct((t, c), BF16),
            jax.ShapeDtypeStruct((t, n_heads), F32),
            jax.ShapeDtypeStruct((t, LANES), F32),
            jax.ShapeDtypeStruct((t // seq, CONV_W - 1, c), F32),
            jax.ShapeDtypeStruct((ts, c), BF16),
            jax.ShapeDtypeStruct((ts, c), F32),
            jax.ShapeDtypeStruct((ts, c), F32),
            jax.ShapeDtypeStruct((ts, c), F32),
            jax.ShapeDtypeStruct((ts, c), F32),
            jax.ShapeDtypeStruct((ts, n_heads), F32),
        ],
        scratch_shapes=[pltpu.VMEM((tm + SUBLANES, c), F32), pltpu.VMEM((ts + SUBLANES, c), F32)],
        compiler_params=_params("arbitrary"),
        name="mixer_in",
    )(x, g, w_main, w_f, b_f, cw, gc, xs, s1, s2)


def _split3(x):
    hi = x.astype(BF16)
    r1 = x - hi.astype(F32)
    mid = r1.astype(BF16)
    lo = (r1 - mid.astype(F32)).astype(BF16)
    return hi, mid, lo


def _attn_prompt_kernel(q_ref, k_ref, v_ref, lfrep_ref, ga_ref, za_ref, aug_ref, o_ref,
                        *, blk, n_heads):
    i = pl.program_id(1)
    seq = k_ref.shape[0]
    lane = lax.broadcasted_iota(jnp.int32, (blk, LANES), 1)
    row = lax.broadcasted_iota(jnp.int32, (blk, blk), 0)
    col = lax.broadcasted_iota(jnp.int32, (blk, blk), 1)
    causal = col <= row

    @pl.when(i == 0)
    def _():
        tri = jnp.where(causal, 1.0, 0.0).astype(BF16)
        carry = jnp.zeros((1, LANES), F32)
        for jb in range(seq // blk):
            rows = slice(jb * blk, (jb + 1) * blk)
            hi, mid, lo = _split3(lfrep_ref[rows, :])
            f = _dot(tri, hi) + _dot(tri, mid) + _dot(tri, lo) + carry
            carry = f[blk - 1:blk, :]
            hi, mid, lo = _split3(f * (-LOG2E))
            zero = jnp.zeros_like(hi)
            aug_ref[rows, :] = jnp.where(
                lane < n_heads, hi,
                jnp.where(lane < 2 * n_heads, mid, jnp.where(lane < 3 * n_heads, lo, zero)))

    def head_out(h, nblk):
        hs = slice(h * HEAD_DIM, (h + 1) * HEAD_DIM)
        ones = jnp.where((lane % n_heads == h) & (lane < 3 * n_heads), 1.0, 0.0).astype(BF16)
        qa = jnp.concatenate([q_ref[:, hs], ones], axis=1)
        scores = []
        for jb in range(nblk):
            rows = slice(jb * blk, (jb + 1) * blk)
            sj = _dot_nt(qa, jnp.concatenate([k_ref[rows, hs], aug_ref[rows, :]], axis=1))
            scores.append(jnp.where(causal, sj, MASK_VALUE) if jb == nblk - 1 else sj)
        mx = scores[0]
        for sj in scores[1:]:
            mx = jnp.maximum(mx, sj)
        m = jnp.max(mx, axis=-1, keepdims=True)
        acc = psum = None
        for jb, sj in enumerate(scores):
            p = jnp.exp2(sj - m)
            o = _dot(p.astype(BF16), v_ref[jb * blk:(jb + 1) * blk, hs])
            psum = p if psum is None else psum + p
            acc = o if acc is None else acc + o
        return acc / jnp.sum(psum, axis=-1, keepdims=True)

    for nblk in range(1, seq // blk + 1):
        @pl.when(i == nblk - 1)
        def _():
            for h in range(n_heads):
                o_ref[:, h * HEAD_DIM:(h + 1) * HEAD_DIM] = head_out(h, nblk)

    za_ref[...] = _rms(o_ref[...], ga_ref[...]).astype(BF16)


def _attn_prompt(q, kb, vb, lfrep, ga, seq, n_heads):
    t, c = q.shape
    b = t // seq
    blk = ATTN_BLOCK
    nq = seq // blk
    kern = functools.partial(_attn_prompt_kernel, blk=blk, n_heads=n_heads)
    return pl.pallas_call(
        kern,
        grid=(b, nq),
        in_specs=[
            pl.BlockSpec((blk, c), lambda bi, i: (bi * nq + i, 0)),
            pl.BlockSpec((seq, c), lambda bi, i: (bi, 0)),
            pl.BlockSpec((seq, c), lambda bi, i: (bi, 0)),
            pl.BlockSpec((seq, LANES), lambda bi, i: (bi, 0)),
            pl.BlockSpec((1, c), lambda bi, i: (0, 0)),
        ],
        out_specs=pl.BlockSpec((blk, c), lambda bi, i: (bi * nq + i, 0)),
        out_shape=jax.ShapeDtypeStruct((t, c), BF16),
        scratch_shapes=[pltpu.VMEM((seq, LANES), BF16), pltpu.VMEM((blk, c), F32)],
        compiler_params=_params("arbitrary", "arbitrary"),
        name="fox_prompt",
    )(q, kb, vb, lfrep, ga)


def _split3_dot(x, w):
    hi, mid, lo = _split3(x)
    return _dot(hi, w) + _dot(mid, w) + _dot(lo, w)


def _select_sum_kernel(x_ref, w_ref, o_ref):
    o_ref[...] = _split3_dot(x_ref[...], w_ref[...])


def _select_sum(x, w, tm):
    rows, k = x.shape
    n = w.shape[1]
    return pl.pallas_call(
        _select_sum_kernel,
        grid=(rows // tm,),
        in_specs=[pl.BlockSpec((tm, k), lambda i: (i, 0)), pl.BlockSpec((k, n), lambda i: (0, 0))],
        out_specs=pl.BlockSpec((tm, n), lambda i: (i, 0)),
        out_shape=jax.ShapeDtypeStruct((rows, n), F32),
        compiler_params=_params("parallel"),
        name="head_major_prefix_sum",
    )(x, w)


def _prefix_matrix(n_steps, n_heads, width):
    r = jnp.arange(n_steps * n_heads)
    cidx = jnp.arange(n_heads * width)
    t, h = r // n_heads, r % n_heads
    h2, t2 = cidx // width, cidx % width
    w = (h[:, None] == h2[None, :]) & (t[:, None] <= t2[None, :]) & (t2[None, :] < n_steps)
    return w.astype(BF16)


def _attn_sample_kernel(pt_ref, q_ref, kn_ref, vn_ref, fn_ref, ga_ref, ck_hbm, cv_hbm, pf_hbm,
                        za_ref, kbuf_ref, vbuf_ref, pfbuf_ref, sem, qbd_ref, kb_ref, vb_ref,
                        m_ref, l_ref, acc_ref, run_ref, *, pages, groups, n_steps, layer, n_heads,
                        dec_seq):
    g = pl.program_id(1)
    step = pl.program_id(0) * groups + g
    slot = step % 2

    def page_copies(step_idx, slot_idx):
        bb = step_idx // groups
        first = (step_idx % groups) * pages
        out = []
        for r in range(pages):
            page = pt_ref[bb, first + r]
            out.append(pltpu.make_async_copy(ck_hbm.at[layer, page], kbuf_ref.at[slot_idx, r],
                                             sem.at[slot_idx, 0]))
            out.append(pltpu.make_async_copy(cv_hbm.at[layer, page], vbuf_ref.at[slot_idx, r],
                                             sem.at[slot_idx, 1]))
            out.append(pltpu.make_async_copy(pf_hbm.at[page], pfbuf_ref.at[slot_idx, r],
                                             sem.at[slot_idx, 2]))
        return out

    @pl.when(step == 0)
    def _():
        for cp in page_copies(0, 0):
            cp.start()

    for n, cp in enumerate(page_copies(jnp.minimum(step + 1, n_steps - 1), 1 - slot)):
        cp.start(priority=1 if n % 3 == 1 else 0)

    for cp in page_copies(step, slot):
        cp.wait()

    rows = dec_seq * n_heads
    c = n_heads * HEAD_DIM
    head_of_row = lax.broadcasted_iota(jnp.int32, (n_heads, c), 0)
    head_of_col = lax.broadcasted_iota(jnp.int32, (n_heads, c), 1) // HEAD_DIM
    diag = head_of_row == head_of_col

    @pl.when(g == 0)
    def _():
        for t in range(dec_seq):
            qt = jnp.broadcast_to(q_ref[0, t:t + 1, :], (n_heads, c))
            qbd_ref[t * n_heads:(t + 1) * n_heads, :] = jnp.where(diag, qt, 0.0).astype(BF16)
        m_ref[...] = jnp.full(m_ref.shape, MASK_VALUE, F32)
        l_ref[...] = jnp.zeros(l_ref.shape, F32)
        acc_ref[...] = jnp.zeros(acc_ref.shape, F32)
        run_ref[...] = jnp.zeros(run_ref.shape, F32)

    def online_update(s, v_bf):
        m_old = m_ref[...]
        m_new = jnp.maximum(m_old, jnp.max(s, axis=-1, keepdims=True))
        alpha = jnp.exp(m_old - m_new)
        p = jnp.exp(s - m_new)
        l_ref[...] = alpha * l_ref[...] + jnp.sum(p, axis=-1, keepdims=True)
        acc_ref[...] = alpha * acc_ref[...] + _dot(p.astype(BF16), v_bf)
        m_ref[...] = m_new

    run = run_ref[...]
    bias_parts = []
    for pg in range(pages):
        toks = slice(pg * PAGE_SIZE, (pg + 1) * PAGE_SIZE)
        for h in range(n_heads):
            hs = slice(h * HEAD_DIM, (h + 1) * HEAD_DIM)
            head_rows = pl.ds(h, PAGE_SIZE, stride=n_heads)
            kb_ref[toks, hs] = kbuf_ref[slot, pg, head_rows, :].astype(BF16)
            vb_ref[toks, hs] = vbuf_ref[slot, pg, head_rows, :].astype(BF16)
        pf = pfbuf_ref[slot, pg]
        bias_parts.append(run + pf)
        run = run + jnp.broadcast_to(pf[:, PAGE_SIZE - 1:PAGE_SIZE], pf.shape)
    run_ref[...] = run
    bias = jnp.concatenate(bias_parts, axis=1)
    bias = jnp.concatenate([bias] * dec_seq, axis=0)
    s = _dot_nt(qbd_ref[...], kb_ref[...]) - bias
    online_update(s, vb_ref[...])

    @pl.when(g == pl.num_programs(1) - 1)
    def _():
        pad = jnp.zeros((PAGE_SIZE - SUBLANES, c), F32)
        kn = jnp.concatenate([kn_ref[0], pad], axis=0).astype(BF16)
        vn = jnp.concatenate([vn_ref[0], pad], axis=0).astype(BF16)
        bias_n = jnp.concatenate([run + fn_ref[0]] * dec_seq, axis=0)
        s_n = _dot_nt(qbd_ref[...], kn) - bias_n
        q_step = lax.broadcasted_iota(jnp.int32, (rows, PAGE_SIZE), 0) // n_heads
        k_step = lax.broadcasted_iota(jnp.int32, (rows, PAGE_SIZE), 1)
        s_n = jnp.where(k_step <= q_step, s_n, MASK_VALUE)
        online_update(s_n, vn)
        o = acc_ref[...] / l_ref[...]
        for t in range(dec_seq):
            ot = jnp.where(diag, o[t * n_heads:(t + 1) * n_heads, :], 0.0)
            ot = jnp.sum(ot, axis=0, keepdims=True)
            za_ref[0, t:t + 1, :] = _rms(ot, ga_ref[...])

    @pl.when(step == n_steps - 1)
    def _():
        for cp in page_copies(n_steps - 1, 1 - slot):
            cp.wait()


def _attn_sample(page_table, q, k_new, v_new, f_new, ga, cache_k, cache_v, pfx, layer, n_heads,
                 dec_seq):
    nb, n_pages = page_table.shape
    c = n_heads * HEAD_DIM
    pages = PAGES_PER_STEP
    groups = n_pages // pages
    rows = dec_seq * n_heads
    per_b = lambda b, g, pt: (b, 0, 0)
    kern = functools.partial(_attn_sample_kernel, pages=pages, groups=groups, n_steps=nb * groups,
                             layer=layer, n_heads=n_heads, dec_seq=dec_seq)
    grid_spec = pltpu.PrefetchScalarGridSpec(
        num_scalar_prefetch=1,
        grid=(nb, groups),
        in_specs=[
            pl.BlockSpec((1, dec_seq, c), per_b),
            pl.BlockSpec((1, SUBLANES, c), per_b),
            pl.BlockSpec((1, SUBLANES, c), per_b),
            pl.BlockSpec((1, n_heads, LANES), per_b),
            pl.BlockSpec((1, c), lambda b, g, pt: (0, 0)),
            pl.BlockSpec(memory_space=pl.ANY),
            pl.BlockSpec(memory_space=pl.ANY),
            pl.BlockSpec(memory_space=pl.ANY),
        ],
        out_specs=pl.BlockSpec((1, dec_seq, c), per_b),
        scratch_shapes=[
            pltpu.VMEM((2, pages, PAGE_SIZE * n_heads, HEAD_DIM), F32),
            pltpu.VMEM((2, pages, PAGE_SIZE * n_heads, HEAD_DIM), F32),
            pltpu.VMEM((2, pages, n_heads, LANES), F32),
            pltpu.SemaphoreType.DMA((2, 3)),
            pltpu.VMEM((rows, c), BF16),
            pltpu.VMEM((pages * PAGE_SIZE, c), BF16),
            pltpu.VMEM((pages * PAGE_SIZE, c), BF16),
            pltpu.VMEM((rows, 1), F32),
            pltpu.VMEM((rows, 1), F32),
            pltpu.VMEM((rows, c), F32),
            pltpu.VMEM((n_heads, LANES), F32),
        ],
    )
    return pl.pallas_call(
        kern,
        grid_spec=grid_spec,
        out_shape=jax.ShapeDtypeStruct((nb, dec_seq, c), F32),
        compiler_params=_params("arbitrary", "arbitrary"),
        name="fox_sample",
    )(page_table, q, k_new, v_new, f_new, ga, cache_k, cache_v, pfx)


def _resident(block_shape, index_map):
    return pl.BlockSpec(block_shape, index_map, pipeline_mode=pl.Buffered(1))


def _mixout_kernel(h_ref, zc_ref, za_ref, hs_ref, zcs_ref, zas_ref, wc_ref, wa_ref, o_ref, os_ref,
                   wcb_ref, wab_ref):
    @pl.when(pl.program_id(0) == 0)
    def _():
        wcb_ref[...] = wc_ref[...].astype(BF16)
        wab_ref[...] = wa_ref[...].astype(BF16)

    def project(h, zc, za):
        return h + _dot(zc.astype(BF16), wcb_ref[...]) + _dot(za.astype(BF16), wab_ref[...])

    o_ref[...] = project(h_ref[...], zc_ref[...], za_ref[...])

    @pl.when(pl.program_id(0) == pl.num_programs(0) - 1)
    def _():
        os_ref[...] = project(hs_ref[...], zcs_ref[...], zas_ref[...])


def _mixout(h, zc, za, hs, zcs, zas, w_out, layer, tm):
    t, d = h.shape
    ts = hs.shape[0]
    c = zc.shape[1]
    assert za.shape[1] == c and w_out.shape[1] == 2 * c
    row = lambda i: (i, 0)
    fixed = lambda i: (0, 0)
    return pl.pallas_call(
        _mixout_kernel,
        grid=(t // tm,),
        in_specs=[pl.BlockSpec((tm, d), row), pl.BlockSpec((tm, c), row), pl.BlockSpec((tm, c), row),
                  _resident((ts, d), fixed), _resident((ts, c), fixed), _resident((ts, c), fixed),
                  _resident((None, c, d), lambda i: (layer, 0, 0)),
                  _resident((None, c, d), lambda i: (layer, 1, 0))],
        out_specs=[pl.BlockSpec((tm, d), row), pl.BlockSpec((ts, d), fixed)],
        out_shape=[jax.ShapeDtypeStruct((t, d), F32), jax.ShapeDtypeStruct((ts, d), F32)],
        scratch_shapes=[pltpu.VMEM((c, d), BF16), pltpu.VMEM((c, d), BF16)],
        compiler_params=_params("arbitrary"),
        name="mixer_out",
    )(h, zc, za, hs, zcs, zas, w_out, w_out)


def _ple_kernel(h_ref, p_ref, hs_ref, ps_ref, g_ref, wg_ref, wp_ref, gf_ref, o_ref, os_ref,
                wgb_ref, wpb_ref, *, final):
    @pl.when(pl.program_id(0) == 0)
    def _():
        wgb_ref[...] = wg_ref[...].astype(BF16)
        wpb_ref[...] = wp_ref[...].astype(BF16)

    def embed(h, p):
        hn = _rms(h, g_ref[...]).astype(BF16)
        gate = jax.nn.sigmoid(_dot(hn, wgb_ref[...]))
        out = h + gate * _dot(p.astype(BF16), wpb_ref[...])
        return _rms(out, gf_ref[...]) if final else out

    o_ref[...] = embed(h_ref[...], p_ref[...])

    @pl.when(pl.program_id(0) == pl.num_programs(0) - 1)
    def _():
        os_ref[...] = embed(hs_ref[...], ps_ref[...])


def _ple(h, p, hs, ps, g, wg, wp, gf, layer, final, tm):
    t, d = h.shape
    ts = hs.shape[0]
    pd = p.shape[1]
    row = lambda i: (i, 0)
    fixed = lambda i: (0, 0)
    return pl.pallas_call(
        functools.partial(_ple_kernel, final=final),
        grid=(t // tm,),
        in_specs=[pl.BlockSpec((tm, d), row), pl.BlockSpec((tm, pd), row),
                  _resident((ts, d), fixed), _resident((ts, pd), fixed),
                  pl.BlockSpec((1, d), fixed),
                  _resident((None, d, d), lambda i: (layer, 0, 0)),
                  _resident((None, pd, d), lambda i: (layer, 0, 0)),
                  pl.BlockSpec((1, d), fixed)],
        out_specs=[pl.BlockSpec((tm, d), row), pl.BlockSpec((ts, d), fixed)],
        out_shape=[jax.ShapeDtypeStruct((t, d), F32), jax.ShapeDtypeStruct((ts, d), F32)],
        scratch_shapes=[pltpu.VMEM((d, d), BF16), pltpu.VMEM((pd, d), BF16)],
        compiler_params=_params("arbitrary"),
        name="ple",
    )(h, p, hs, ps, g, wg, wp, gf)


def kernel(x_prompt, x_sample, p_prompt, p_sample, cache_k, cache_v, cache_logf, state_conv, page_table, norm_ffn1, w_ffn1_gate, w_ffn1_up, w_ffn1_down, norm_mix, w_in, b_f, conv_w, norm_conv_out, norm_attn_out, w_out, norm_ffn2, w_ffn2_gate, w_ffn2_up, w_ffn2_down, norm_ple, w_ple_gate, w_ple_proj, norm_final):
    batch, seq, d = x_prompt.shape
    dec_batch, dec_seq, _ = x_sample.shape
    depth = w_in.shape[0]
    n_heads = b_f.shape[1]
    conv_dim = conv_w.shape[2]
    attn_dim = n_heads * HEAD_DIM
    n_pool = cache_k.shape[1]
    scale = HEAD_DIM ** -0.5
    n_main = 3 * conv_dim + 3 * attn_dim
    assert conv_dim == attn_dim and dec_seq >= CONV_W - 1 and dec_seq <= SUBLANES
    tp, ts = batch * seq, dec_batch * dec_seq

    hp = x_prompt.reshape(tp, d)
    hs = x_sample.reshape(ts, d)
    row2 = lambda a: a.reshape(1, -1)
    page_prefix_w = _prefix_matrix(PAGE_SIZE, n_heads, PAGE_SIZE)
    new_prefix_w = _prefix_matrix(dec_seq, n_heads, LANES)

    outs = [[] for _ in range(8)]
    for l in range(depth):
        w_main = w_in[l].astype(BF16)
        w_f = jnp.tile(w_in[l, :, n_main:], (1, LANES // n_heads)).astype(BF16)
        g1, gm, g2, gp = row2(norm_ffn1[l]), row2(norm_mix[l]), row2(norm_ffn2[l]), row2(norm_ple[l])
        gc, ga = row2(norm_conv_out[l]), row2(norm_attn_out[l])
        bf = row2(jnp.tile(b_f[l], LANES // n_heads))
        cw = conv_w[l]
        last = l == depth - 1
        gfin = row2(norm_final)

        hp, hs = _ffn(hp, hs, g1, w_ffn1_gate, w_ffn1_up, w_ffn1_down, l)

        st = state_conv[l]
        zero = jnp.zeros((dec_batch, dec_seq - 1, conv_dim), F32)
        s1 = jnp.concatenate([st[:, 1:2], zero], axis=1).reshape(ts, conv_dim)
        s2 = jnp.concatenate([st, zero[:, 1:]], axis=1).reshape(ts, conv_dim)
        (zc_p, q_p, k_p, v_p, kb_p, vb_p, lf_p, lfrep_p, cs_p, zc_s, q_s, k_s, v_s, u_s, lf_s) = _mixin(
            hp, hs, gm, w_main, w_f, bf, cw, gc, s1, s2, seq, dec_seq, n_heads, scale * LOG2E, scale)

        za_p = _attn_prompt(q_p, kb_p, vb_p, lfrep_p, ga, seq, n_heads)

        pfx = _select_sum(cache_logf[l].reshape(n_pool, PAGE_SIZE * n_heads), page_prefix_w, 512)
        pfx = pfx.reshape(n_pool, n_heads, PAGE_SIZE)
        f_new = _select_sum(lf_s.reshape(dec_batch, dec_seq * n_heads), new_prefix_w, dec_batch)
        f_new = f_new.reshape(dec_batch, n_heads, LANES)
        pad8 = lambda a: jnp.pad(a.reshape(dec_batch, dec_seq, attn_dim),
                                 ((0, 0), (0, SUBLANES - dec_seq), (0, 0)))
        za_s = _attn_sample(page_table, q_s.reshape(dec_batch, dec_seq, attn_dim), pad8(k_s),
                            pad8(v_s), f_new, ga,
                            cache_k.reshape(depth, n_pool, PAGE_SIZE * n_heads, HEAD_DIM),
                            cache_v.reshape(depth, n_pool, PAGE_SIZE * n_heads, HEAD_DIM),
                            pfx, l, n_heads, dec_seq)

        hp, hs = _mixout(hp, zc_p, za_p, hs, zc_s, za_s.reshape(ts, attn_dim), w_out, l, TOKEN_TILE)
        hp, hs = _ffn(hp, hs, g2, w_ffn2_gate, w_ffn2_up, w_ffn2_down, l)
        hp, hs = _ple(hp, p_prompt[l].reshape(tp, -1), hs, p_sample[l].reshape(ts, -1), gp,
                      w_ple_gate, w_ple_proj, gfin, l, last, TOKEN_TILE)

        outs[0].append(k_p.reshape(batch, seq, n_heads, HEAD_DIM))
        outs[1].append(v_p.reshape(batch, seq, n_heads, HEAD_DIM))
        outs[2].append(lf_p.reshape(batch, seq, n_heads))
        outs[3].append(cs_p)
        outs[4].append(k_s.reshape(dec_batch, dec_seq, n_heads, HEAD_DIM))
        outs[5].append(v_s.reshape(dec_batch, dec_seq, n_heads, HEAD_DIM))
        outs[6].append(lf_s.reshape(dec_batch, dec_seq, n_heads))
        outs[7].append(u_s.reshape(dec_batch, dec_seq, conv_dim)[:, dec_seq - (CONV_W - 1):])

    return (hp.reshape(batch, seq, d), hs.reshape(dec_batch, dec_seq, d),
            *(jnp.stack(o) for o in outs))
```
